```python
import math
import jax, jax.numpy as jnp
from jax import lax
import numpy as np

D_MODEL = 1024
BATCH = 16
SEQ = 2048
DEPTH = 1
DEC_BATCH = 128
DEC_SEQ = 4
PAST_LEN = 8192
PAGE_SIZE = 128

NSA_HEADS = 8
NSA_KV_HEADS = 2
NSA_GROUP = NSA_HEADS // NSA_KV_HEADS
NSA_DH = 64
CMP_BLOCK = 32
CMP_STRIDE = 16
CMP_HIDDEN = 128
SLC_BLOCK = 64
SLC_TOPK = 16
WINDOW = 512
Q_BLOCK = 128
HG_HEADS = 4
HG_DK = 128
HG_DV = 128
HG_CHUNK = 64
N_EXPERTS = 32
TOP_K = 4
D_EXPERT = D_MODEL
SWIGLU_LIMIT = 7.0
SWIGLU_ALPHA = 1.702
EPS = 1e-6
NEG_INF = -1e30
FORCE_SCORE = 1e4
Q_W = NSA_HEADS * NSA_DH
KV_W = NSA_KV_HEADS * NSA_DH
NSA_GATE_W = 3 * NSA_HEADS
HG_QK_W = HG_HEADS * HG_DK
HG_V_W = HG_HEADS * HG_DV
SPLIT_SIZES = (Q_W, KV_W, KV_W, KV_W, KV_W, KV_W, KV_W, NSA_GATE_W, HG_QK_W, HG_QK_W, HG_V_W, HG_V_W, D_MODEL, D_MODEL)
IN_W = sum(SPLIT_SIZES)

kernel_name = 'nsa_hgrn2_gated_moe_decode_step'


def rms_norm(x, g):
    xf = x.astype(jnp.float32)
    y = xf * lax.rsqrt(jnp.mean(xf * xf, axis=-1, keepdims=True) + EPS)
    return (y * g.astype(jnp.float32)).astype(x.dtype)


def modulation(c, w_ada, b_ada):
    mod = jax.nn.silu(c) @ w_ada + b_ada
    return [m[:, None, :] for m in jnp.split(mod, 6, axis=-1)]


def split_projection(z):
    offsets, acc = [], 0
    for s in SPLIT_SIZES[:-1]:
        acc += s
        offsets.append(acc)
    return jnp.split(z, offsets, axis=-1)


def alibi_slopes():
    return jnp.exp2(-8.0 * jnp.arange(1, NSA_HEADS + 1, dtype=jnp.float32) / NSA_HEADS).reshape(NSA_KV_HEADS, NSA_GROUP)


def compress(k, pe, w1, w2):
    B, L, G, dh = k.shape
    r = CMP_BLOCK // CMP_STRIDE
    n_chunks = L // CMP_STRIDE
    n_cmp = n_chunks - r + 1
    chunks = k[:, :n_chunks * CMP_STRIDE].reshape(B, n_chunks, CMP_STRIDE, G, dh)
    chunks = chunks.transpose(0, 1, 3, 2, 4).reshape(B, n_chunks, G, CMP_STRIDE * dh)
    w1r = w1.reshape(r, CMP_STRIDE * dh, CMP_HIDDEN)
    pre = pe.reshape(-1) @ w1
    for j in range(r):
        pre = pre + jnp.einsum('bngf,fh->bngh', chunks[:, j:j + n_cmp], w1r[j])
    return jax.nn.gelu(pre) @ w2


def compressed_attend(q, kcb, vcb, q_pos, n_slc, slopes):
    n_cmp = kcb.shape[1]
    start = jnp.arange(n_cmp, dtype=jnp.int32) * CMP_STRIDE
    dist = (q_pos[:, None] - (start + CMP_BLOCK - 1)[None, :]).astype(jnp.float32)
    valid = (dist >= 0)[:, None, None, :]
    s = jnp.einsum('btgrd,bngd->btgrn', q, kcb).astype(jnp.float32) * (NSA_DH ** -0.5)
    s = jnp.where(valid, s - slopes[:, :, None] * dist[:, None, None, :], NEG_INF)
    p = jnp.where(valid, jax.nn.softmax(s, axis=-1), 0.0)
    o = jnp.einsum('btgrn,bngd->btgrd', p.astype(vcb.dtype), vcb)
    slc_start = jnp.arange(n_slc, dtype=jnp.int32) * SLC_BLOCK
    overlap = ((start[:, None] < slc_start[None, :] + SLC_BLOCK) & (start[:, None] + CMP_BLOCK > slc_start[None, :])).astype(jnp.float32)
    imp = jnp.einsum('btgrn,nj->btgj', p, overlap)
    return o, imp


def select_blocks(imp, q_pos, n_slc):
    j = jnp.arange(n_slc, dtype=jnp.int32)
    cur = q_pos // SLC_BLOCK
    forced = (j[None, :] == 0) | (j[None, :] == cur[:, None]) | (j[None, :] == cur[:, None] - 1)
    causal = j[None, :] <= cur[:, None]
    score = jnp.where(forced[None, :, None, :], FORCE_SCORE, jnp.where(causal[None, :, None, :], imp, -FORCE_SCORE))
    _, idx = lax.top_k(score, min(SLC_TOPK, n_slc))
    return idx


def sparse_attend(q, k, v, k_pos, q_pos, slopes):
    dist = (q_pos[:, None, None] - k_pos).astype(jnp.float32)
    valid = (dist >= 0)[..., None, :]
    s = jnp.einsum('...tgrd,...tgnd->...tgrn', q, k).astype(jnp.float32) * (NSA_DH ** -0.5)
    s = jnp.where(valid, s - slopes[:, :, None] * dist[..., None, :], NEG_INF)
    p = jax.nn.softmax(s, axis=-1)
    return jnp.einsum('...tgrn,...tgnd->...tgrd', p.astype(v.dtype), v)


def window_attend(q, k, v, q_pos, k_pos, slopes):
    dist = q_pos[:, None] - k_pos[None, :]
    valid = ((dist >= 0) & (dist < WINDOW) & (k_pos[None, :] >= 0))[:, None, None, :]
    s = jnp.einsum('btgrd,bngd->btgrn', q, k).astype(jnp.float32) * (NSA_DH ** -0.5)
    s = jnp.where(valid, s - slopes[:, :, None] * dist.astype(jnp.float32)[:, None, None, :], NEG_INF)
    p = jax.nn.softmax(s, axis=-1)
    return jnp.einsum('btgrn,bngd->btgrd', p.astype(v.dtype), v)


def prompt_selected(q, k, v, idx, slopes):
    B, T, G, R, dh = q.shape
    offs = jnp.arange(SLC_BLOCK, dtype=jnp.int32)
    gidx = jnp.arange(G, dtype=jnp.int32)[None, :, None]

    def per_seq(args):
        qb, kb, vb, ib = args

        def per_block(n):
            start = n * Q_BLOCK
            qs = lax.dynamic_slice_in_dim(qb, start, Q_BLOCK, axis=0)
            isel = lax.dynamic_slice_in_dim(ib, start, Q_BLOCK, axis=0)
            tok = (isel[..., None] * SLC_BLOCK + offs).reshape(Q_BLOCK, G, -1)
            q_pos = start + jnp.arange(Q_BLOCK, dtype=jnp.int32)
            return sparse_attend(qs, kb[tok, gidx], vb[tok, gidx], tok, q_pos, slopes)

        out = lax.map(per_block, jnp.arange(T // Q_BLOCK, dtype=jnp.int32))
        return out.reshape(T, G, R, dh)

    return lax.map(per_seq, (q, k, v, idx))


def prompt_window(q, k, v, slopes):
    B, T, G, R, dh = q.shape
    pad = jnp.zeros((B, WINDOW, G, dh), k.dtype)
    kp = jnp.concatenate([pad, k], axis=1)
    vp = jnp.concatenate([pad, v], axis=1)
    span = WINDOW + Q_BLOCK

    def per_block(n):
        start = n * Q_BLOCK
        qs = lax.dynamic_slice_in_dim(q, start, Q_BLOCK, axis=1)
        kb = lax.dynamic_slice_in_dim(kp, start, span, axis=1)
        vb = lax.dynamic_slice_in_dim(vp, start, span, axis=1)
        q_pos = start + jnp.arange(Q_BLOCK, dtype=jnp.int32)
        k_pos = start - WINDOW + jnp.arange(span, dtype=jnp.int32)
        return window_attend(qs, kb, vb, q_pos, k_pos, slopes)

    out = lax.map(per_block, jnp.arange(T // Q_BLOCK, dtype=jnp.int32))
    return out.transpose(1, 0, 2, 3, 4, 5).reshape(B, T, G, R, dh)


def nsa_combine(o_cmp, o_slc, o_win, g_nsa):
    B, T = g_nsa.shape[:2]
    g = jax.nn.sigmoid(g_nsa).reshape(B, T, 3, NSA_KV_HEADS, NSA_GROUP, 1)
    o = g[:, :, 0] * o_cmp + g[:, :, 1] * o_slc + g[:, :, 2] * o_win
    return o.reshape(B, T, Q_W)


def hgrn2(hq, hf, hi, hg, lb, S0, norm_w):
    B, T, _ = hq.shape
    q = jax.nn.silu(hq.astype(jnp.float32)).reshape(B, T, HG_HEADS, HG_DK)
    lbh = lb.reshape(HG_HEADS, HG_DK)
    f = lbh + (1.0 - lbh) * jax.nn.sigmoid(hf.astype(jnp.float32).reshape(B, T, HG_HEADS, HG_DK))
    k = 1.0 - f
    g = jnp.log(f)
    v = hi.astype(jnp.float32).reshape(B, T, HG_HEADS, HG_DV)
    C = math.gcd(T, HG_CHUNK)
    n = T // C

    def to_chunks(a):
        return a.reshape(B, n, C, HG_HEADS, a.shape[-1]).transpose(1, 0, 3, 2, 4)

    tri = jnp.tril(jnp.ones((C, C), dtype=bool))

    def step(S, inp):
        qc, kc, gc, vc = inp
        b = jnp.cumsum(gc, axis=2)
        o_inter = jnp.einsum('bhtk,bhkv->bhtv', qc * jnp.exp(b), S)
        rel = b[:, :, :, None, :] - b[:, :, None, :, :]
        decay = jnp.exp(jnp.where(tri[:, :, None], rel, -jnp.inf))
        A = jnp.einsum('bhtk,bhsk,bhtsk->bhts', qc, kc, decay)
        o_intra = jnp.einsum('bhts,bhsv->bhtv', A, vc)
        bC = b[:, :, -1]
        S = jnp.exp(bC)[..., None] * S + jnp.einsum('bhsk,bhsv->bhkv', kc * jnp.exp(bC[:, :, None] - b), vc)
        return S, o_inter + o_intra

    S, o = lax.scan(step, S0.astype(jnp.float32), (to_chunks(q), to_chunks(k), to_chunks(g), to_chunks(v)))
    o = o.transpose(1, 0, 3, 2, 4).reshape(B, T, HG_HEADS, HG_DV)
    o = o * lax.rsqrt(jnp.mean(o * o, axis=-1, keepdims=True) + EPS) * norm_w.astype(jnp.float32)
    o = o * jax.nn.silu(hg.astype(jnp.float32).reshape(B, T, HG_HEADS, HG_DV))
    return o.reshape(B, T, HG_V_W).astype(hq.dtype), S.astype(S0.dtype)


def merge_branches(o_nsa, o_hg, ga, gb, w_branch_a, w_branch_b, w_out):
    m = jax.nn.sigmoid(ga) * (o_nsa @ w_branch_a) + jax.nn.sigmoid(gb) * (o_hg @ w_branch_b)
    return m @ w_out


def mixer_prompt(h, w_in, cmp_pe, cmp_w1, cmp_w2, lb, hg_norm, w_branch_a, w_branch_b, w_out, slopes):
    B, T, _ = h.shape
    q, kc, vc, ks, vs, kw, vw, g_nsa, hq, hf, hi, hg, ga, gb = split_projection(h @ w_in)
    q = q.reshape(B, T, NSA_KV_HEADS, NSA_GROUP, NSA_DH)
    kc, vc, ks, vs, kw, vw = [a.reshape(B, T, NSA_KV_HEADS, NSA_DH) for a in (kc, vc, ks, vs, kw, vw)]
    q_pos = jnp.arange(T, dtype=jnp.int32)
    n_slc = -(-T // SLC_BLOCK)
    kcb = compress(kc, cmp_pe[0], cmp_w1[0], cmp_w2[0])
    vcb = compress(vc, cmp_pe[1], cmp_w1[1], cmp_w2[1])
    o_cmp, imp = compressed_attend(q, kcb, vcb, q_pos, n_slc, slopes)
    idx = select_blocks(imp, q_pos, n_slc)
    o_slc = prompt_selected(q, ks, vs, idx, slopes)
    o_win = prompt_window(q, kw, vw, slopes)
    o_nsa = nsa_combine(o_cmp, o_slc, o_win, g_nsa)
    S0 = jnp.zeros((B, HG_HEADS, HG_DK, HG_DV), h.dtype)
    o_hg, S = hgrn2(hq, hf, hi, hg, lb, S0, hg_norm)
    out = merge_branches(o_nsa, o_hg, ga, gb, w_branch_a, w_branch_b, w_out)
    keep = min(WINDOW, T)
    kv_cmp = jnp.stack([kc, vc], axis=2)
    kv_sel = jnp.stack([ks, vs], axis=2)
    win = jnp.stack([kw, vw], axis=2)[:, T - keep:]
    return out, kv_cmp, kv_sel, win, S


def mixer_sample(h, cache_cmp, cache_sel, win_buf, S0, page_table, w_in, cmp_pe, cmp_w1, cmp_w2, lb, hg_norm, w_branch_a, w_branch_b, w_out, slopes):
    B, T, _ = h.shape
    L = PAST_LEN + T
    q, kc, vc, ks, vs, kw, vw, g_nsa, hq, hf, hi, hg, ga, gb = split_projection(h @ w_in)
    q = q.reshape(B, T, NSA_KV_HEADS, NSA_GROUP, NSA_DH)
    kc, vc, ks, vs, kw, vw = [a.reshape(B, T, NSA_KV_HEADS, NSA_DH) for a in (kc, vc, ks, vs, kw, vw)]
    q_pos = PAST_LEN + jnp.arange(T, dtype=jnp.int32)
    past_cmp = cache_cmp[page_table].reshape(B, PAST_LEN, 2, NSA_KV_HEADS, NSA_DH)
    kc_all = jnp.concatenate([past_cmp[:, :, 0], kc], axis=1)
    vc_all = jnp.concatenate([past_cmp[:, :, 1], vc], axis=1)
    kcb = compress(kc_all, cmp_pe[0], cmp_w1[0], cmp_w2[0])
    vcb = compress(vc_all, cmp_pe[1], cmp_w1[1], cmp_w2[1])
    n_slc = -(-L // SLC_BLOCK)
    o_cmp, imp = compressed_attend(q, kcb, vcb, q_pos, n_slc, slopes)
    idx = select_blocks(imp, q_pos, n_slc)
    tok = (idx[..., None] * SLC_BLOCK + jnp.arange(SLC_BLOCK, dtype=jnp.int32)).reshape(B, T, NSA_KV_HEADS, -1)
    bi = jnp.arange(B, dtype=jnp.int32)[:, None, None, None]
    gi = jnp.arange(NSA_KV_HEADS, dtype=jnp.int32)[None, None, :, None]
    tp = jnp.minimum(tok, PAST_LEN - 1)
    phys = page_table[bi, tp // PAGE_SIZE]
    off = tp % PAGE_SIZE
    tn = jnp.clip(tok - PAST_LEN, 0, T - 1)
    in_past = (tok < PAST_LEN)[..., None]
    k_sel = jnp.where(in_past, cache_sel[phys, off, 0, gi], ks[bi, tn, gi])
    v_sel = jnp.where(in_past, cache_sel[phys, off, 1, gi], vs[bi, tn, gi])
    o_slc = sparse_attend(q, k_sel, v_sel, tok, q_pos, slopes)
    WB = win_buf.shape[1]
    buf = jnp.concatenate([win_buf, jnp.stack([kw, vw], axis=2)], axis=1)
    k_pos = PAST_LEN - WB + jnp.arange(WB + T, dtype=jnp.int32)
    o_win = window_attend(q, buf[:, :, 0], buf[:, :, 1], q_pos, k_pos, slopes)
    o_nsa = nsa_combine(o_cmp, o_slc, o_win, g_nsa)
    o_hg, S = hgrn2(hq, hf, hi, hg, lb, S0, hg_norm)
    out = merge_branches(o_nsa, o_hg, ga, gb, w_branch_a, w_branch_b, w_out)
    kv_cmp = jnp.stack([kc, vc], axis=2)
    kv_sel = jnp.stack([ks, vs], axis=2)
    return out, kv_cmp, kv_sel, buf[:, T:], S


def moe_ffn(h, w_router, b_router, w_up, b_up, w_down, b_down):
    B, T, D = h.shape
    xt = h.reshape(-1, D)
    logits = (xt @ w_router + b_router).astype(jnp.float32)
    top_v, top_i = lax.top_k(logits, TOP_K)
    wts = jax.nn.softmax(top_v, axis=-1)
    gate_dense = jnp.sum(jax.nn.one_hot(top_i, N_EXPERTS, dtype=jnp.float32) * wts[..., None], axis=1).astype(h.dtype)
    out = jnp.zeros_like(xt)
    for e in range(N_EXPERTS):
        z = xt @ w_up[e] + b_up[e]
        gate = jnp.minimum(z[:, :D_EXPERT], SWIGLU_LIMIT)
        up = jnp.clip(z[:, D_EXPERT:], -SWIGLU_LIMIT, SWIGLU_LIMIT)
        act = (up + 1.0) * gate * jax.nn.sigmoid(SWIGLU_ALPHA * gate)
        out = out + gate_dense[:, e:e + 1] * (act @ w_down[e] + b_down[e])
    return out.reshape(B, T, D)


def setup_inputs(seed: int = 0) -> dict:
    key = jax.random.key(seed)
    ks = jax.random.split(key, 32)
    f32 = jnp.float32
    n_pages = PAST_LEN // PAGE_SIZE
    n_pool = (DEC_BATCH * n_pages * 5) // 4
    win_buf = min(WINDOW, PAST_LEN)

    def nrm(k, shape, scale):
        return jax.random.normal(k, shape, f32) * scale

    page_table = jax.random.permutation(ks[8], n_pool)[: DEC_BATCH * n_pages].reshape(DEC_BATCH, n_pages).astype(jnp.int32)
    return {
        'x_prompt': nrm(ks[0], (BATCH, SEQ, D_MODEL), 1.0),
        'x_sample': nrm(ks[1], (DEC_BATCH, DEC_SEQ, D_MODEL), 1.0),
        'c_prompt': nrm(ks[2], (BATCH, D_MODEL), 1.0),
        'c_sample': nrm(ks[3], (DEC_BATCH, D_MODEL), 1.0),
        'cache_cmp': nrm(ks[4], (DEPTH, n_pool, PAGE_SIZE, 2, NSA_KV_HEADS, NSA_DH), 1.0),
        'cache_sel': nrm(ks[5], (DEPTH, n_pool, PAGE_SIZE, 2, NSA_KV_HEADS, NSA_DH), 1.0),
        'state_win': nrm(ks[6], (DEPTH, DEC_BATCH, win_buf, 2, NSA_KV_HEADS, NSA_DH), 1.0),
        'state_hgrn': nrm(ks[7], (DEPTH, DEC_BATCH, HG_HEADS, HG_DK, HG_DV), 0.5),
        'page_table': page_table,
        'w_ada': nrm(ks[9], (DEPTH, D_MODEL, 6 * D_MODEL), 0.5 * D_MODEL ** -0.5),
        'b_ada': nrm(ks[10], (DEPTH, 6 * D_MODEL), 0.02),
        'norm_mix_pre': 1.0 + nrm(ks[11], (DEPTH, D_MODEL), 0.05),
        'norm_mix_post': 1.0 + nrm(ks[12], (DEPTH, D_MODEL), 0.05),
        'norm_ffn_pre': 1.0 + nrm(ks[13], (DEPTH, D_MODEL), 0.05),
        'norm_ffn_post': 1.0 + nrm(ks[14], (DEPTH, D_MODEL), 0.05),
        'w_in': nrm(ks[15], (DEPTH, D_MODEL, IN_W), D_MODEL ** -0.5),
        'cmp_pe': nrm(ks[16], (DEPTH, 2, CMP_BLOCK, NSA_DH), 0.1),
        'cmp_w1': nrm(ks[17], (DEPTH, 2, CMP_BLOCK * NSA_DH, CMP_HIDDEN), (CMP_BLOCK * NSA_DH) ** -0.5),
        'cmp_w2': nrm(ks[18], (DEPTH, 2, CMP_HIDDEN, NSA_DH), CMP_HIDDEN ** -0.5),
        'hg_lb_logits': nrm(ks[19], (DEPTH + 1, HG_QK_W), 0.5),
        'hg_norm': 1.0 + nrm(ks[20], (DEPTH, HG_DV), 0.05),
        'w_branch_a': nrm(ks[21], (DEPTH, Q_W, D_MODEL), Q_W ** -0.5),
        'w_branch_b': nrm(ks[22], (DEPTH, HG_V_W, D_MODEL), HG_V_W ** -0.5),
        'w_out': nrm(ks[23], (DEPTH, D_MODEL, D_MODEL), D_MODEL ** -0.5),
        'w_router': nrm(ks[24], (DEPTH, D_MODEL, N_EXPERTS), D_MODEL ** -0.5),
        'b_router': nrm(ks[25], (DEPTH, N_EXPERTS), 0.01),
        'w_up': nrm(ks[26], (DEPTH, N_EXPERTS, D_MODEL, 2 * D_EXPERT), D_MODEL ** -0.5),
        'b_up': nrm(ks[27], (DEPTH, N_EXPERTS, 2 * D_EXPERT), 0.02),
        'w_down': nrm(ks[28], (DEPTH, N_EXPERTS, D_EXPERT, D_MODEL), D_EXPERT ** -0.5),
        'b_down': nrm(ks[29], (DEPTH, N_EXPERTS, D_MODEL), 0.02),
    }


def reference(x_prompt, x_sample, c_prompt, c_sample, cache_cmp, cache_sel, state_win, state_hgrn, page_table,
              w_ada, b_ada, norm_mix_pre, norm_mix_post, norm_ffn_pre, norm_ffn_post, w_in, cmp_pe, cmp_w1, cmp_w2,
              hg_lb_logits, hg_norm, w_branch_a, w_branch_b, w_out, w_router, b_router, w_up, b_up, w_down, b_down):
    slopes = alibi_slopes()
    lb_all = jnp.cumsum(jax.nn.softmax(hg_lb_logits.astype(jnp.float32), axis=0), axis=0)
    yp, ys = x_prompt, x_sample
    cmp_p, cmp_s, sel_p, sel_s, win_p, win_s, hg_p, hg_s = [], [], [], [], [], [], [], []
    for l in range(DEPTH):
        shared = (w_in[l], cmp_pe[l], cmp_w1[l], cmp_w2[l], lb_all[l], hg_norm[l], w_branch_a[l], w_branch_b[l], w_out[l], slopes)
        moe_w = (w_router[l], b_router[l], w_up[l], b_up[l], w_down[l], b_down[l])
        sh1, sc1, g1, sh2, sc2, g2 = modulation(c_prompt, w_ada[l], b_ada[l])
        h = rms_norm(yp, norm_mix_pre[l]) * (1.0 + sc1) + sh1
        out, kvc, kvs, win, S = mixer_prompt(h, *shared)
        yp = yp + g1 * rms_norm(out, norm_mix_post[l])
        h = rms_norm(yp, norm_ffn_pre[l]) * (1.0 + sc2) + sh2
        yp = yp + g2 * rms_norm(moe_ffn(h, *moe_w), norm_ffn_post[l])
        cmp_p.append(kvc)
        sel_p.append(kvs)
        win_p.append(win)
        hg_p.append(S)
        sh1, sc1, g1, sh2, sc2, g2 = modulation(c_sample, w_ada[l], b_ada[l])
        h = rms_norm(ys, norm_mix_pre[l]) * (1.0 + sc1) + sh1
        out, kvc, kvs, win, S = mixer_sample(h, cache_cmp[l], cache_sel[l], state_win[l], state_hgrn[l], page_table, *shared)
        ys = ys + g1 * rms_norm(out, norm_mix_post[l])
        h = rms_norm(ys, norm_ffn_pre[l]) * (1.0 + sc2) + sh2
        ys = ys + g2 * rms_norm(moe_ffn(h, *moe_w), norm_ffn_post[l])
        cmp_s.append(kvc)
        sel_s.append(kvs)
        win_s.append(win)
        hg_s.append(S)
    return (yp, ys, jnp.stack(cmp_p), jnp.stack(cmp_s), jnp.stack(sel_p), jnp.stack(sel_s), jnp.stack(win_p), jnp.stack(win_s), jnp.stack(hg_p), jnp.stack(hg_s))
```

```python
import functools

import numpy as np
import jax
import jax.numpy as jnp
from jax import lax
from jax.experimental import pallas as pl
from jax.experimental.pallas import tpu as pltpu

F32, BF16, I32 = jnp.float32, jnp.bfloat16, jnp.int32

NSA_HEADS, NSA_KV, NSA_GROUP, DH = 8, 2, 4, 64
CMP_BLOCK, CMP_STRIDE, CMP_HIDDEN = 32, 16, 128
SLC_BLOCK, SLC_TOPK, WINDOW = 64, 16, 512
HG_HEADS, HG_DK, HG_DV, HG_CHUNK = 4, 128, 128, 64
N_EXPERTS, TOP_K = 32, 4
SWIGLU_LIMIT, SWIGLU_ALPHA = 7.0, 1.702
EPS, NEG, FORCE = 1e-6, -1e30, 1e4
PAGE = 128

C_GA, C_GB, C_Q, C_KC, C_KS, C_KW, C_GN = 0, 1024, 2048, 2560, 2816, 3072, 3328
C_HQ, C_HF, C_HI, C_HG, ZW = 3456, 3968, 4480, 4992, 5632
SUB = 8

VMEM_LIMIT = 56 * 1024 * 1024


def _cp(sem, vmem=VMEM_LIMIT):
    return pltpu.CompilerParams(dimension_semantics=sem, vmem_limit_bytes=vmem)


def _dot(a, b):
    return jnp.dot(a, b, preferred_element_type=F32)


def _dot_nt(a, b):
    return lax.dot_general(a, b, (((1,), (1,)), ((), ())), preferred_element_type=F32)


def _dot_tn(a, b):
    return lax.dot_general(a, b, (((0,), (0,)), ((), ())), preferred_element_type=F32)


def _split3(x):
    a = x.astype(BF16)
    r = x - a.astype(F32)
    b = r.astype(BF16)
    c = (r - b.astype(F32)).astype(BF16)
    return a, b, c


def _dot3(x, w_bf16):
    a, b, c = _split3(x)
    return _dot(a, w_bf16) + _dot(b, w_bf16) + _dot(c, w_bf16)


def _sigmoid(x):
    return 1.0 / (1.0 + jnp.exp(-x))


def _silu(x):
    return x * _sigmoid(x)


def _rms(x, g):
    return x * lax.rsqrt(jnp.mean(x * x, axis=-1, keepdims=True) + EPS) * g


def _gelu_tanh(x):
    return 0.5 * x * (1.0 + jnp.tanh(0.7978845608028654 * (x + 0.044715 * (x * x * x))))


def _mod_kernel(c_ref, w_ref, b_ref, o_ref):
    o_ref[...] = _dot(_silu(c_ref[...]).astype(BF16), w_ref[...].astype(BF16)) + b_ref[...]


def _modulation(c, w_ada, b_ada):
    m, d = c.shape
    nw = w_ada.shape[1]
    tn = nw // 6
    return pl.pallas_call(
        _mod_kernel,
        grid=(nw // tn,),
        in_specs=[pl.BlockSpec((m, d), lambda j: (0, 0)),
                  pl.BlockSpec((d, tn), lambda j: (0, j)),
                  pl.BlockSpec((1, tn), lambda j: (0, j))],
        out_specs=pl.BlockSpec((m, tn), lambda j: (0, j)),
        out_shape=jax.ShapeDtypeStruct((m, nw), F32),
        compiler_params=_cp(("arbitrary",)),
        name="modulation",
    )(c, w_ada, b_ada.reshape(1, nw))


def _inproj_kernel(x_ref, g_ref, sc_ref, sh_ref, w_ref, o_ref, h_scr):
    @pl.when(pl.program_id(1) == 0)
    def _():
        h = _rms(x_ref[...], g_ref[...]) * (1.0 + sc_ref[...]) + sh_ref[...]
        h_scr[...] = h.astype(BF16)

    o_ref[...] = _dot(h_scr[...], w_ref[...])


def _inproj(x2d, g, sc, sh, w_packed, rows_per_mod, tm, tn=512):
    n, d = x2d.shape
    zw = w_packed.shape[1]
    per = rows_per_mod // tm
    mod_spec = pl.BlockSpec((None, 1, d), lambda i, j: (i // per, 0, 0))
    return pl.pallas_call(
        _inproj_kernel,
        grid=(n // tm, zw // tn),
        in_specs=[pl.BlockSpec((tm, d), lambda i, j: (i, 0)),
                  pl.BlockSpec((1, d), lambda i, j: (0, 0)),
                  mod_spec, mod_spec,
                  pl.BlockSpec((d, tn), lambda i, j: (0, j))],
        out_specs=pl.BlockSpec((tm, tn), lambda i, j: (i, j)),
        out_shape=jax.ShapeDtypeStruct((n, zw), F32),
        scratch_shapes=[pltpu.VMEM((tm, d), BF16)],
        compiler_params=_cp(("arbitrary", "arbitrary")),
        name="inproj",
    )(x2d, g.reshape(1, d), sc, sh, w_packed)


def _inproj_rows(x2d, g, sc_rows, sh_rows, w_packed, tm, tn=512):
    n, d = x2d.shape
    zw = w_packed.shape[1]
    return pl.pallas_call(
        _inproj_kernel,
        grid=(n // tm, zw // tn),
        in_specs=[pl.BlockSpec((tm, d), lambda i, j: (i, 0)),
                  pl.BlockSpec((1, d), lambda i, j: (0, 0)),
                  pl.BlockSpec((tm, d), lambda i, j: (i, 0)),
                  pl.BlockSpec((tm, d), lambda i, j: (i, 0)),
                  pl.BlockSpec((d, tn), lambda i, j: (0, j))],
        out_specs=pl.BlockSpec((tm, tn), lambda i, j: (i, j)),
        out_shape=jax.ShapeDtypeStruct((n, zw), F32),
        scratch_shapes=[pltpu.VMEM((tm, d), BF16)],
        compiler_params=_cp(("arbitrary", "arbitrary")),
        name="inproj_rows",
    )(x2d, g.reshape(1, d), sc_rows, sh_rows, w_packed)


def _compress(load, nch, w1_ref, pe_ref, w2_ref):
    out = []
    for kv in range(2):
        pe_acc = None
        for tok in range(CMP_STRIDE):
            t = _dot(pe_ref[kv, tok].astype(BF16), w1_ref[kv, tok])
            pe_acc = t if pe_acc is None else pe_acc + t
        pre0 = pe_acc[0:1, :CMP_HIDDEN] + pe_acc[1:2, CMP_HIDDEN:]
        w2 = w2_ref[kv].astype(BF16)
        row = []
        for g in range(2):
            acc = None
            for tok in range(CMP_STRIDE):
                t = _dot(load(tok, kv)[:, g * DH:(g + 1) * DH].astype(BF16), w1_ref[kv, tok])
                acc = t if acc is None else acc + t
            pre = acc[:, :CMP_HIDDEN] + pltpu.roll(acc[:, CMP_HIDDEN:], nch - 1, 0) + pre0
            row.append(_dot(_gelu_tanh(pre).astype(BF16), w2))
        out.append(row)
    return out


def _compress_prompt_kernel(k_ref, v_ref, w1_ref, pe_ref, w2_ref, o_ref, *, nch):
    def load(tok, kv):
        return (k_ref, v_ref)[kv][pl.ds(tok, nch, stride=CMP_STRIDE), :]

    out = _compress(load, nch, w1_ref, pe_ref, w2_ref)
    for kv in range(2):
        for g in range(2):
            o_ref[kv, g] = out[kv][g]


def _compress_prompt(z3, w1t, pe_t, w2):
    b, t, _ = z3.shape
    nch = t // CMP_STRIDE
    return pl.pallas_call(
        functools.partial(_compress_prompt_kernel, nch=nch),
        grid=(b,),
        in_specs=[pl.BlockSpec((None, t, 128), lambda i: (i, 0, C_KC // 128)),
                  pl.BlockSpec((None, t, 128), lambda i: (i, 0, C_KC // 128 + 1)),
                  pl.BlockSpec(w1t.shape, lambda i: (0, 0, 0, 0)),
                  pl.BlockSpec(pe_t.shape, lambda i: (0, 0, 0, 0)),
                  pl.BlockSpec(w2.shape, lambda i: (0, 0, 0))],
        out_specs=pl.BlockSpec((None, 2, 2, nch, DH), lambda i: (i, 0, 0, 0, 0)),
        out_shape=jax.ShapeDtypeStruct((b, 2, 2, nch, DH), F32),
        compiler_params=_cp(("arbitrary",)),
        name="compress_prompt",
    )(z3, z3, w1t, pe_t, w2)


def _slope(g, r):
    return 2.0 ** (-(g * NSA_GROUP + r + 1))


def _stack_heads(q, g):
    parts = [q[:, (g * NSA_GROUP + r) * DH:(g * NSA_GROUP + r + 1) * DH] for r in range(NSA_GROUP)]
    return (jnp.concatenate(parts, axis=0) * (DH ** -0.5)).astype(BF16)


def _head_cols(tq, q0, g):
    pos1 = q0 + lax.broadcasted_iota(I32, (tq, 1), 0)
    qpos = jnp.concatenate([pos1] * NSA_GROUP, axis=0)
    slope = jnp.concatenate([jnp.full((tq, 1), _slope(g, r), F32) for r in range(NSA_GROUP)], axis=0)
    return qpos, -slope * qpos.astype(F32), slope


def _masked_scores(q4, k_bf16, kpos_row, valid, a_col, s_col):
    s = _dot_nt(q4, k_bf16) + (a_col + s_col * kpos_row.astype(F32))
    return jnp.where(valid, s, NEG)


def _topk_mask(score, k):
    lane = lax.broadcasted_iota(I32, score.shape, 1).astype(F32)
    sel = jnp.zeros(score.shape, F32)
    for _ in range(k):
        mx = jnp.max(score, axis=-1, keepdims=True)
        idx = jnp.min(jnp.where(score == mx, lane, 1e9), axis=-1, keepdims=True)
        pick = lane == idx
        sel = jnp.where(pick, 1.0, sel)
        score = jnp.where(pick, -jnp.inf, score)
    return sel


def _cmp_branch(q4, kcb, vcb, ov_ref, qpos, a_col, s_col, tq, q0, n_slc):
    ncmp = kcb.shape[0]
    endpos = lax.broadcasted_iota(I32, (1, ncmp), 1) * CMP_STRIDE + (CMP_BLOCK - 1)
    valid = endpos <= qpos
    s = _masked_scores(q4, kcb.astype(BF16), endpos, valid, a_col, s_col)
    m = jnp.max(s, axis=-1, keepdims=True)
    e = jnp.exp(s - m)
    p = jnp.where(valid, e, 0.0) / jnp.sum(e, axis=-1, keepdims=True)
    o = _dot(p.astype(BF16), vcb.astype(BF16))
    psum = p[0:tq] + p[tq:2 * tq] + p[2 * tq:3 * tq] + p[3 * tq:4 * tq]
    imp = _dot3(psum, ov_ref[...])
    jl = lax.broadcasted_iota(I32, imp.shape, 1)
    cur = lax.shift_right_logical(q0 + lax.broadcasted_iota(I32, (tq, 1), 0), 6)
    forced = (jl == 0) | (jl == cur) | (jl == cur - 1)
    score = jnp.where(forced, FORCE, jnp.where(jl <= cur, imp, -FORCE))
    score = jnp.where(jl < n_slc, score, -3e38)
    return o, _topk_mask(score, min(SLC_TOPK, n_slc))


def _gate_combine(gn, outs, tq, g):
    sig = _sigmoid(gn)
    heads = []
    for r in range(NSA_GROUP):
        acc = None
        for br, o in enumerate(outs):
            c = br * NSA_HEADS + g * NSA_GROUP + r
            t = sig[:, c:c + 1] * o[r * tq:(r + 1) * tq, :]
            acc = t if acc is None else acc + t
        heads.append(acc)
    return jnp.concatenate(heads, axis=1)


def _nsa_prompt_kernel(q_ref, gn_ref, kvc_ref, ks_ref, vs_ref, kw_ref, vw_ref, ov_ref, e_ref, o_ref,
                       *, tq, tk, t_len, n_slc):
    i = pl.program_id(1)
    q0 = i * tq
    q = q_ref[...]
    gn = gn_ref[...]
    span = min(WINDOW + tq, t_len)
    for g in range(NSA_KV):
        lanes = slice(g * DH, (g + 1) * DH)
        q4 = _stack_heads(q, g)
        qpos, a_col, s_col = _head_cols(tq, q0, g)
        o_cmp, sel = _cmp_branch(q4, kvc_ref[0, g], kvc_ref[1, g], ov_ref, qpos, a_col, s_col, tq, q0, n_slc)
        sel4 = jnp.concatenate([sel] * NSA_GROUP, axis=0).astype(BF16)

        def body(kt, carry):
            m, l, acc = carry
            k0 = pl.multiple_of(kt * tk, tk)
            kpos = k0 + lax.broadcasted_iota(I32, (1, tk), 1)
            picked = _dot(sel4, e_ref[:, pl.ds(k0, tk)]) > 0.5
            valid = picked & (kpos <= qpos)
            s = _masked_scores(q4, ks_ref[pl.ds(k0, tk), lanes].astype(BF16), kpos, valid, a_col, s_col)
            m_new = jnp.maximum(m, jnp.max(s, axis=-1, keepdims=True))
            alpha = jnp.exp(m - m_new)
            p = jnp.exp(s - m_new)
            l = alpha * l + jnp.sum(p, axis=-1, keepdims=True)
            acc = alpha * acc + _dot(p.astype(BF16), vs_ref[pl.ds(k0, tk), lanes].astype(BF16))
            return m_new, l, acc

        init = (jnp.full((4 * tq, 1), NEG, F32), jnp.zeros((4 * tq, 1), F32), jnp.zeros((4 * tq, DH), F32))
        _, l, acc = lax.fori_loop(0, (q0 + tq + tk - 1) // tk, body, init)
        o_slc = acc / l

        w0 = pl.multiple_of(jnp.maximum(q0 + tq - span, 0), 128)
        kpos = w0 + lax.broadcasted_iota(I32, (1, span), 1)
        valid = (kpos <= qpos) & (kpos > qpos - WINDOW)
        s = _masked_scores(q4, kw_ref[pl.ds(w0, span), lanes].astype(BF16), kpos, valid, a_col, s_col)
        p = jnp.exp(s - jnp.max(s, axis=-1, keepdims=True))
        o_win = _dot(p.astype(BF16), vw_ref[pl.ds(w0, span), lanes].astype(BF16)) / jnp.sum(p, axis=-1, keepdims=True)

        o_ref[:, g * 256:(g + 1) * 256] = _gate_combine(gn, (o_cmp, o_slc, o_win), tq, g)


def _nsa_prompt(z3, kvcb, ov, e_mat, tq, tk):
    b, t, _ = z3.shape
    n_slc = -(-t // SLC_BLOCK)
    nq = t // tq
    kv_spec = lambda col: pl.BlockSpec((None, t, 128), lambda bi, i, c=col // 128: (bi, 0, c))
    return pl.pallas_call(
        functools.partial(_nsa_prompt_kernel, tq=tq, tk=tk, t_len=t, n_slc=n_slc),
        grid=(b, nq),
        in_specs=[pl.BlockSpec((None, tq, 512), lambda bi, i: (bi, i, C_Q // 512)),
                  pl.BlockSpec((None, tq, 128), lambda bi, i: (bi, i, C_GN // 128)),
                  pl.BlockSpec((None,) + kvcb.shape[1:], lambda bi, i: (bi, 0, 0, 0, 0)),
                  kv_spec(C_KS), kv_spec(C_KS + 128), kv_spec(C_KW), kv_spec(C_KW + 128),
                  pl.BlockSpec(ov.shape, lambda bi, i: (0, 0)),
                  pl.BlockSpec(e_mat.shape, lambda bi, i: (0, 0))],
        out_specs=pl.BlockSpec((None, tq, 512), lambda bi, i: (bi, i, 0)),
        out_shape=jax.ShapeDtypeStruct((b, t, 512), F32),
        compiler_params=_cp(("arbitrary", "arbitrary")),
        name="nsa_prompt",
    )(z3, z3, kvcb, z3, z3, z3, z3, ov, e_mat)


def _nsa_sample_kernel(pt_ref, q_ref, gn_ref, kvs_ref, kvw_ref, win_ref, cmp_hbm, sel_hbm,
                       w1_ref, pe_ref, w2_ref, ov_ref, e_ref, o_ref,
                       cmp_buf, sel_buf, new_scr, sem, *, n_pages, past, t_new):
    b = pl.program_id(0)
    nb = pl.num_programs(0)
    slot = b % 2
    nch = past // CMP_STRIDE
    wb = win_ref.shape[0]

    def page_copies(bb, sl, pg):
        rows = pl.ds(pg * PAGE, PAGE)
        p = pt_ref[bb, pg]
        return (pltpu.make_async_copy(cmp_hbm.at[p, :, 0, :], cmp_buf.at[sl, 0, rows, :], sem.at[0, sl]),
                pltpu.make_async_copy(cmp_hbm.at[p, :, 1, :], cmp_buf.at[sl, 1, rows, :], sem.at[0, sl]),
                pltpu.make_async_copy(sel_hbm.at[p], sel_buf.at[sl, rows, :], sem.at[1, sl]))

    def fetch(bb, sl):
        for pg in range(n_pages):
            for c in page_copies(bb, sl, pg):
                c.start()

    @pl.when(b == 0)
    def _():
        fetch(0, 0)

    @pl.when(b + 1 < nb)
    def _():
        fetch(b + 1, 1 - slot)

    for pg in range(n_pages):
        for c in page_copies(b, slot, pg):
            c.wait()

    new_scr[...] = jnp.zeros(new_scr.shape, F32)
    new_scr[0:SUB, 0:256] = kvs_ref[...]
    new_scr[0:SUB, 256:512] = kvw_ref[...]

    def load(tok, kv):
        return cmp_buf[slot, kv, pl.ds(tok, nch, stride=CMP_STRIDE), :]

    kvcb = _compress(load, nch, w1_ref, pe_ref, w2_ref)

    tq = SUB
    n_slc = -(-(past + t_new) // SLC_BLOCK)
    q = q_ref[...]
    gn = gn_ref[...]
    npos = past + lax.broadcasted_iota(I32, (1, 128), 1)
    ppos = lax.broadcasted_iota(I32, (1, past), 1)
    wpos = past - wb + lax.broadcasted_iota(I32, (1, wb), 1)
    for g in range(NSA_KV):
        kl = slice(g * DH, (g + 1) * DH)
        vl = slice(128 + g * DH, 128 + (g + 1) * DH)
        q4 = _stack_heads(q, g)
        qpos, a_col, s_col = _head_cols(tq, past, g)
        o_cmp, sel = _cmp_branch(q4, kvcb[0][g], kvcb[1][g], ov_ref, qpos, a_col, s_col, tq, past, n_slc)
        sel4 = jnp.concatenate([sel[:, :128]] * NSA_GROUP, axis=0).astype(BF16)

        picked = _dot(sel4, e_ref[...]) > 0.5
        s_p = _masked_scores(q4, sel_buf[slot, :, kl].astype(BF16), ppos, picked & (ppos <= qpos), a_col, s_col)
        s_n = _masked_scores(q4, new_scr[:, kl].astype(BF16), npos, npos <= qpos, a_col, s_col)
        m = jnp.maximum(jnp.max(s_p, axis=-1, keepdims=True), jnp.max(s_n, axis=-1, keepdims=True))
        p_p = jnp.exp(s_p - m)
        p_n = jnp.exp(s_n - m)
        l = jnp.sum(p_p, axis=-1, keepdims=True) + jnp.sum(p_n, axis=-1, keepdims=True)
        o_slc = (_dot(p_p.astype(BF16), sel_buf[slot, :, vl].astype(BF16))
                 + _dot(p_n.astype(BF16), new_scr[:, vl].astype(BF16))) / l

        kl2 = slice(256 + g * DH, 256 + (g + 1) * DH)
        vl2 = slice(384 + g * DH, 384 + (g + 1) * DH)
        s_p = _masked_scores(q4, win_ref[:, kl].astype(BF16), wpos, (wpos <= qpos) & (wpos > qpos - WINDOW),
                             a_col, s_col)
        s_n = _masked_scores(q4, new_scr[:, kl2].astype(BF16), npos, npos <= qpos, a_col, s_col)
        m = jnp.maximum(jnp.max(s_p, axis=-1, keepdims=True), jnp.max(s_n, axis=-1, keepdims=True))
        p_p = jnp.exp(s_p - m)
        p_n = jnp.exp(s_n - m)
        l = jnp.sum(p_p, axis=-1, keepdims=True) + jnp.sum(p_n, axis=-1, keepdims=True)
        o_win = (_dot(p_p.astype(BF16), win_ref[:, vl].astype(BF16))
                 + _dot(p_n.astype(BF16), new_scr[:, vl2].astype(BF16))) / l

        o_ref[:, g * 256:(g + 1) * 256] = _gate_combine(gn, (o_cmp, o_slc, o_win), tq, g)


def _nsa_sample(z3, win, cache_cmp, cache_sel, page_table, w1t, pe_t, w2, ov, e_mat, t_new):
    bs = z3.shape[0]
    n_pages = page_table.shape[1]
    past = n_pages * PAGE
    wb = win.shape[1]
    const = lambda shape: pl.BlockSpec(shape, lambda bi, pt, n=len(shape): (0,) * n)
    grid_spec = pltpu.PrefetchScalarGridSpec(
        num_scalar_prefetch=1,
        grid=(bs,),
        in_specs=[pl.BlockSpec((None, SUB, 512), lambda bi, pt: (bi, 0, C_Q // 512)),
                  pl.BlockSpec((None, SUB, 128), lambda bi, pt: (bi, 0, C_GN // 128)),
                  pl.BlockSpec((None, SUB, 256), lambda bi, pt: (bi, 0, C_KS // 256)),
                  pl.BlockSpec((None, SUB, 256), lambda bi, pt: (bi, 0, C_KW // 256)),
                  pl.BlockSpec((None, wb, 256), lambda bi, pt: (bi, 0, 0)),
                  pl.BlockSpec(memory_space=pl.ANY),
                  pl.BlockSpec(memory_space=pl.ANY),
                  const(w1t.shape), const(pe_t.shape), const(w2.shape), const(ov.shape), const(e_mat.shape)],
        out_specs=pl.BlockSpec((None, SUB, 512), lambda bi, pt: (bi, 0, 0)),
        scratch_shapes=[pltpu.VMEM((2, 2, past, 128), F32),
                        pltpu.VMEM((2, past, 256), F32),
                        pltpu.VMEM((128, 512), F32),
                        pltpu.SemaphoreType.DMA((2, 2))],
    )
    return pl.pallas_call(
        functools.partial(_nsa_sample_kernel, n_pages=n_pages, past=past, t_new=t_new),
        grid_spec=grid_spec,
        out_shape=jax.ShapeDtypeStruct((bs, SUB, 512), F32),
        compiler_params=_cp(("arbitrary",), 60 * 1024 * 1024),
        name="nsa_sample",
    )(page_table, z3, z3, z3, z3, win, cache_cmp, cache_sel, w1t, pe_t, w2, ov, e_mat)


def _hgrn_kernel(hq_ref, hf_ref, hi_ref, hg_ref, lb_ref, nw_ref, s0_ref, o_ref, sout_ref, st_scr, o_scr,
                 *, t_len, chunk, t_real):
    st_scr[...] = s0_ref[...].T
    lb = lb_ref[...]
    nw = nw_ref[...]
    row = lax.broadcasted_iota(I32, (chunk, 1), 0)
    real = row < t_real
    tri = (lax.broadcasted_iota(I32, (chunk, chunk), 0) >= lax.broadcasted_iota(I32, (chunk, chunk), 1)).astype(BF16)
    blk = 8

    def step(ci, carry):
        rows = pl.ds(pl.multiple_of(ci * chunk, chunk), chunk)
        q = _silu(hq_ref[rows, :])
        f = lb + (1.0 - lb) * _sigmoid(hf_ref[rows, :])
        k = jnp.where(real, 1.0 - f, 0.0)
        gl = jnp.where(real, jnp.log(f), 0.0)
        v = hi_ref[rows, :]
        if chunk > blk:
            ga_, gb_, gc_ = _split3(gl)
            bcum = _dot(tri, ga_) + _dot(tri, gb_) + _dot(tri, gc_)
        else:
            bcum = jnp.zeros_like(gl)
            for s in range(t_real):
                bcum = bcum + jnp.where(row >= s, gl[s:s + 1, :], 0.0)
        st = st_scr[...]
        o_scr[...] = _dot_nt((q * jnp.exp(bcum)).astype(BF16), st.astype(BF16))
        for lo in range(0, min(chunk, t_real), blk):
            qj = q[lo:, :]
            bj = bcum[lo:, :]
            tj = row[lo:, :]
            acc = jnp.zeros((chunk - lo, HG_DV), F32)
            for s in range(lo, min(lo + blk, t_real)):
                d = jnp.where(tj >= s, bj - bcum[s:s + 1, :], NEG)
                w = jnp.sum(qj * jnp.exp(d) * k[s:s + 1, :], axis=-1, keepdims=True)
                acc = acc + w * v[s:s + 1, :]
            o_scr[lo:, :] = o_scr[lo:, :] + acc
        bc = bcum[chunk - 1:chunk, :]
        kt = k * jnp.exp(bc - bcum)
        st_scr[...] = st * jnp.exp(bc) + _dot_tn(v.astype(BF16), kt.astype(BF16))
        o = o_scr[...]
        o = o * lax.rsqrt(jnp.mean(o * o, axis=-1, keepdims=True) + EPS) * nw
        o_ref[rows, :] = o * _silu(hg_ref[rows, :])
        return carry

    lax.fori_loop(0, t_len // chunk, step, 0)
    sout_ref[...] = st_scr[...].T


def _hgrn(z3, lb, norm_w, s0, chunk, t_real):
    b, t, _ = z3.shape
    col = lambda c0: pl.BlockSpec((None, t, 128), lambda bi, h, c=c0 // 128: (bi, 0, c + h))
    return pl.pallas_call(
        functools.partial(_hgrn_kernel, t_len=t, chunk=chunk, t_real=t_real),
        grid=(b, HG_HEADS),
        in_specs=[col(C_HQ), col(C_HF), col(C_HI), col(C_HG),
                  pl.BlockSpec((None, 1, HG_DK), lambda bi, h: (h, 0, 0)),
                  pl.BlockSpec((1, HG_DV), lambda bi, h: (0, 0)),
                  pl.BlockSpec((None, None, HG_DK, HG_DV), lambda bi, h: (bi, h, 0, 0))],
        out_specs=[pl.BlockSpec((None, t, 128), lambda bi, h: (bi, 0, h)),
                   pl.BlockSpec((None, None, HG_DK, HG_DV), lambda bi, h: (bi, h, 0, 0))],
        out_shape=[jax.ShapeDtypeStruct((b, t, HG_HEADS * HG_DV), F32),
                   jax.ShapeDtypeStruct((b, HG_HEADS, HG_DK, HG_DV), F32)],
        scratch_shapes=[pltpu.VMEM((HG_DV, HG_DK), F32), pltpu.VMEM((chunk, HG_DV), F32)],
        compiler_params=_cp(("arbitrary", "arbitrary")),
        name="hgrn2",
    )(z3, z3, z3, z3, lb.reshape(HG_HEADS, 1, HG_DK), norm_w.reshape(1, HG_DV), s0)


def _merge_kernel(x_ref, on_ref, oh_ref, ga_ref, gb_ref, wa_ref, wb_ref, wo_ref, g1_ref, sc_ref, sh_ref,
                  npost_ref, npre_ref, wr_ref, br_ref, y_ref, h_ref, ti_ref, tw_ref):
    a = _dot(on_ref[...].astype(BF16), wa_ref[...])
    bb = _dot(oh_ref[...].astype(BF16), wb_ref[...])
    m = _sigmoid(ga_ref[...]) * a + _sigmoid(gb_ref[...]) * bb
    out = _dot(m.astype(BF16), wo_ref[...])
    y = x_ref[...] + g1_ref[...] * _rms(out, npost_ref[...])
    y_ref[...] = y
    h = _rms(y, npre_ref[...]) * (1.0 + sc_ref[...]) + sh_ref[...]
    h_ref[...] = h
    hh, hl, _ = _split3(h)
    w = wr_ref[...]
    wh = w.astype(BF16)
    wl = (w - wh.astype(F32)).astype(BF16)
    logits = _dot(hh, wh) + _dot(hh, wl) + _dot(hl, wh) + br_ref[...]
    lane = lax.broadcasted_iota(I32, logits.shape, 1)
    lanef = lane.astype(F32)
    logits = jnp.where(lane < N_EXPERTS, logits, -jnp.inf)
    ti = jnp.zeros(logits.shape, F32)
    tv = jnp.zeros(logits.shape, F32)
    v0 = None
    for kk in range(TOP_K):
        mx = jnp.max(logits, axis=-1, keepdims=True)
        idx = jnp.min(jnp.where(logits == mx, lanef, 1e9), axis=-1, keepdims=True)
        v0 = mx if v0 is None else v0
        ti = jnp.where(lane == kk, idx, ti)
        tv = jnp.where(lane == kk, jnp.exp(mx - v0), tv)
        logits = jnp.where(lanef == idx, -jnp.inf, logits)
    ti_ref[...] = ti.astype(I32)
    tw_ref[...] = tv / jnp.sum(tv, axis=-1, keepdims=True)


def _merge(x2d, o_nsa, o_hg, z2d, g1, sc2, sh2, wa, wb, wo, n_post, n_pre, w_router, b_router, rows_per_mod, tm):
    n, d = x2d.shape
    row = lambda w: pl.BlockSpec((tm, w), lambda i: (i, 0))
    if rows_per_mod:
        per = rows_per_mod // tm
        mod_spec = pl.BlockSpec((None, 1, d), lambda i: (i // per, 0, 0))
    else:
        mod_spec = row(d)
    const2 = lambda a: pl.BlockSpec(a.shape, lambda i: (0, 0))
    wr = jnp.pad(w_router, ((0, 0), (0, 128 - N_EXPERTS)))
    br = jnp.pad(b_router, (0, 128 - N_EXPERTS)).reshape(1, 128)
    vec = lambda v: v.reshape(1, d)
    return pl.pallas_call(
        _merge_kernel,
        grid=(n // tm,),
        in_specs=[row(d), row(512), row(512),
                  pl.BlockSpec((tm, d), lambda i: (i, C_GA // 1024)),
                  pl.BlockSpec((tm, d), lambda i: (i, C_GB // 1024)),
                  const2(wa), const2(wb), const2(wo),
                  mod_spec, mod_spec, mod_spec,
                  pl.BlockSpec((1, d), lambda i: (0, 0)), pl.BlockSpec((1, d), lambda i: (0, 0)),
                  const2(wr), const2(br)],
        out_specs=[row(d), row(d), row(128), row(128)],
        out_shape=[jax.ShapeDtypeStruct((n, d), F32), jax.ShapeDtypeStruct((n, d), F32),
                   jax.ShapeDtypeStruct((n, 128), I32), jax.ShapeDtypeStruct((n, 128), F32)],
        compiler_params=_cp(("arbitrary",)),
        name="merge_router",
    )(x2d, o_nsa, o_hg, z2d, z2d, wa, wb, wo, g1, sc2, sh2, vec(n_post), vec(n_pre), wr, br)


def _moe_kernel(te_ref, nu_ref, tok_ref, tok_next_ref, rw_ref, h_hbm, wu_ref, bu_ref, wd_ref, bd_ref, o_ref,
                xbuf, sem, *, tm):
    t = pl.program_id(0)
    n_used = nu_ref[0]
    slot = t % 2

    def gather(idx_ref, sl):
        def body(r, c):
            pltpu.make_async_copy(h_hbm.at[pl.ds(idx_ref[r], 1), :], xbuf.at[sl, pl.ds(r, 1), :], sem.at[sl]).start()
            return c
        lax.fori_loop(0, tm, body, 0, unroll=8)

    @pl.when((t == 0) & (n_used > 0))
    def _():
        gather(tok_ref, 0)

    @pl.when(t + 1 < n_used)
    def _():
        gather(tok_next_ref, 1 - slot)

    @pl.when(t < n_used)
    def _():
        pltpu.make_async_copy(h_hbm.at[pl.ds(0, tm), :], xbuf.at[slot], sem.at[slot]).wait()
        x = xbuf[slot].astype(BF16)
        z = _dot(x, wu_ref[...]) + bu_ref[...]
        de = wd_ref.shape[0]
        gate = jnp.minimum(z[:, :de], SWIGLU_LIMIT)
        up = jnp.clip(z[:, de:], -SWIGLU_LIMIT, SWIGLU_LIMIT)
        act = (up + 1.0) * gate * _sigmoid(SWIGLU_ALPHA * gate)
        y = _dot(act.astype(BF16), wd_ref[...]) + bd_ref[...]
        o_ref[...] = y * rw_ref[...]

    @pl.when(t >= n_used)
    def _():
        o_ref[...] = jnp.zeros(o_ref.shape, F32)


def _moe_ffn(h2, tile_expert, n_used, row_token, row_weight, w_up, b_up, w_down, b_down, tm):
    n, d = h2.shape
    n_tiles = tile_expert.shape[0]
    ne, _, dh2 = w_up.shape
    grid_spec = pltpu.PrefetchScalarGridSpec(
        num_scalar_prefetch=2,
        grid=(n_tiles,),
        in_specs=[pl.BlockSpec((tm,), lambda t, te, nu: (t,), memory_space=pltpu.SMEM),
                  pl.BlockSpec((tm,), lambda t, te, nu: (jnp.minimum(t + 1, n_tiles - 1),), memory_space=pltpu.SMEM),
                  pl.BlockSpec((tm, 1), lambda t, te, nu: (t, 0)),
                  pl.BlockSpec(memory_space=pl.ANY),
                  pl.BlockSpec((None, d, dh2), lambda t, te, nu: (te[t], 0, 0)),
                  pl.BlockSpec((None, 1, dh2), lambda t, te, nu: (te[t], 0, 0)),
                  pl.BlockSpec((None, dh2 // 2, d), lambda t, te, nu: (te[t], 0, 0)),
                  pl.BlockSpec((None, 1, d), lambda t, te, nu: (te[t], 0, 0))],
        out_specs=pl.BlockSpec((tm, d), lambda t, te, nu: (t, 0)),
        scratch_shapes=[pltpu.VMEM((2, tm, d), F32), pltpu.SemaphoreType.DMA((2,))],
    )
    return pl.pallas_call(
        functools.partial(_moe_kernel, tm=tm),
        grid_spec=grid_spec,
        out_shape=jax.ShapeDtypeStruct((n_tiles * tm, d), F32),
        compiler_params=_cp(("arbitrary",)),
        name="moe_ffn",
    )(tile_expert, n_used, row_token, row_token, row_weight, h2, w_up, b_up.reshape(ne, 1, dh2),
      w_down, b_down.reshape(ne, 1, d))


def _combine_kernel(pos_ref, y_hbm, y1_ref, g2_ref, npost_ref, o_ref, buf, sem, *, tm):
    def body(r, c):
        for kk in range(TOP_K):
            pltpu.make_async_copy(y_hbm.at[pl.ds(pos_ref[r * TOP_K + kk], 1), :], buf.at[kk, pl.ds(r, 1), :], sem.at[0]).start()
        return c
    lax.fori_loop(0, tm, body, 0, unroll=4)
    for kk in range(TOP_K):
        pltpu.make_async_copy(y_hbm.at[pl.ds(0, tm), :], buf.at[kk], sem.at[0]).wait()
    moe = (buf[0] + buf[1]) + (buf[2] + buf[3])
    o_ref[...] = y1_ref[...] + g2_ref[...] * _rms(moe, npost_ref[...])


def _combine(pos_flat, y_sorted, y1, g2_rows, n_post, tm):
    n, d = y1.shape
    return pl.pallas_call(
        functools.partial(_combine_kernel, tm=tm),
        grid=(n // tm,),
        in_specs=[pl.BlockSpec((tm * TOP_K,), lambda i: (i,), memory_space=pltpu.SMEM),
                  pl.BlockSpec(memory_space=pl.ANY),
                  pl.BlockSpec((tm, d), lambda i: (i, 0)),
                  pl.BlockSpec((tm, d), lambda i: (i, 0)),
                  pl.BlockSpec((1, d), lambda i: (0, 0))],
        out_specs=pl.BlockSpec((tm, d), lambda i: (i, 0)),
        out_shape=jax.ShapeDtypeStruct((n, d), F32),
        scratch_shapes=[pltpu.VMEM((TOP_K, tm, d), F32), pltpu.SemaphoreType.DMA((1,))],
        compiler_params=_cp(("arbitrary",)),
        name="moe_combine",
    )(pos_flat, y_sorted, y1, g2_rows, n_post.reshape(1, d))


def _route_tables(top_i, top_w, tm):
    n = top_i.shape[0]
    e_flat = top_i.reshape(-1)
    order = jnp.argsort(e_flat, stable=True).astype(I32)
    e_sorted = e_flat[order]
    counts = jnp.zeros((N_EXPERTS,), I32).at[e_flat].add(1)
    tiles_per = (counts + tm - 1) // tm
    ustart = jnp.cumsum(counts) - counts
    tstart = jnp.cumsum(tiles_per) - tiles_per
    n_tiles = (n * TOP_K) // tm + N_EXPERTS
    pos = tstart[e_sorted] * tm + (jnp.arange(n * TOP_K, dtype=I32) - ustart[e_sorted])
    row_token = jnp.zeros((n_tiles * tm,), I32).at[pos].set(order // TOP_K)
    row_weight = jnp.zeros((n_tiles * tm,), F32).at[pos].set(top_w.reshape(-1)[order])
    pos_of = jnp.zeros((n * TOP_K,), I32).at[order].set(pos)
    n_used = jnp.sum(tiles_per).astype(I32)
    tile_ids = jnp.arange(n_tiles, dtype=I32)
    tile_expert = jnp.sum((tile_ids[:, None] >= (tstart + tiles_per)[None, :]).astype(I32), axis=1)
    tile_expert = jnp.minimum(tile_expert, N_EXPERTS - 1).astype(I32)
    return tile_expert, n_used.reshape(1), row_token, row_weight.reshape(-1, 1), pos_of


def _overlap_table(n_rows, n_cmp, n_cols):
    start = np.arange(n_rows)[:, None] * CMP_STRIDE
    j0 = np.arange(n_cols)[None, :] * SLC_BLOCK
    ov = (start < j0 + SLC_BLOCK) & (start + CMP_BLOCK > j0) & (np.arange(n_rows)[:, None] < n_cmp)
    return jnp.asarray(ov.astype(np.float32), dtype=BF16)


def _expand_table(n_rows, n_keys):
    e = (np.arange(n_keys)[None, :] // SLC_BLOCK) == np.arange(n_rows)[:, None]
    return jnp.asarray(e.astype(np.float32), dtype=BF16)


def _pack_w_in(w_in):
    d = w_in.shape[0]
    q = w_in[:, 0:512]
    kv = w_in[:, 512:1280]
    gn = w_in[:, 1280:1304]
    hh = w_in[:, 1304:3352]
    ga = w_in[:, 3352:4376]
    gb = w_in[:, 4376:5400]
    z = lambda w: jnp.zeros((d, w), w_in.dtype)
    return jnp.concatenate([ga, gb, q, kv, gn, z(128 - 24), hh, z(ZW - C_HG - 512)], axis=1).astype(BF16)


def _pack_compress(cmp_pe, cmp_w1, cmp_w2):
    r = CMP_BLOCK // CMP_STRIDE
    w1t = cmp_w1.reshape(2, r, CMP_STRIDE, DH, CMP_HIDDEN).transpose(0, 2, 3, 1, 4).reshape(2, CMP_STRIDE, DH, r * CMP_HIDDEN)
    pe_t = cmp_pe.reshape(2, r, CMP_STRIDE, DH).transpose(0, 2, 1, 3)
    pe_t = jnp.pad(pe_t, ((0, 0), (0, 0), (0, SUB - r), (0, 0)))
    return w1t.astype(BF16), pe_t, cmp_w2


def _kv_out(z3, c0, rows):
    b = z3.shape[0]
    return z3[:, rows, c0:c0 + 256].reshape(b, -1, 2, NSA_KV, DH)[None]


def kernel(x_prompt, x_sample, c_prompt, c_sample, cache_cmp, cache_sel, state_win, state_hgrn, page_table, w_ada, b_ada, norm_mix_pre, norm_mix_post, norm_ffn_pre, norm_ffn_post, w_in, cmp_pe, cmp_w1, cmp_w2, hg_lb_logits, hg_norm, w_branch_a, w_branch_b, w_out, w_router, b_router, w_up, b_up, w_down, b_down):
    bp, t, d = x_prompt.shape
    bs, ts, _ = x_sample.shape
    depth = w_in.shape[0]
    assert depth == 1 and ts <= SUB
    n_pool = cache_cmp.shape[1]
    past = page_table.shape[1] * PAGE
    lb_all = jnp.cumsum(jax.nn.softmax(hg_lb_logits.astype(F32), axis=0), axis=0)

    l = 0
    w_packed = _pack_w_in(w_in[l])
    w1t, pe_t, w2c = _pack_compress(cmp_pe[l], cmp_w1[l], cmp_w2[l])
    wa, wb, wo = w_branch_a[l].astype(BF16), w_branch_b[l].astype(BF16), w_out[l].astype(BF16)
    wu, wd = w_up[l].astype(BF16), w_down[l].astype(BF16)

    mod = _modulation(jnp.concatenate([c_prompt, c_sample], axis=0), w_ada[l], b_ada[l])
    mods = [m[:, None, :] for m in jnp.split(mod, 6, axis=-1)]
    sh1, sc1, g1, sh2, sc2, g2 = mods
    p_, s_ = slice(0, bp), slice(bp, bp + bs)

    tm_p = 1024 if t % 1024 == 0 else t
    zp = _inproj(x_prompt.reshape(bp * t, d), norm_mix_pre[l], sc1[p_], sh1[p_], w_packed, t, tm_p)
    zp3 = zp.reshape(bp, t, ZW)
    nch = t // CMP_STRIDE
    n_slc = -(-t // SLC_BLOCK)
    kvcb_p = _compress_prompt(zp3, w1t, pe_t, w2c)
    tq = 256 if t % 256 == 0 else t
    tk = 512 if t % 512 == 0 else t
    o_nsa_p = _nsa_prompt(zp3, kvcb_p, _overlap_table(nch, nch - 1, 128), _expand_table(128, t), tq, tk)
    s0_p = jnp.zeros((bp, HG_HEADS, HG_DK, HG_DV), F32)
    chunk = int(np.gcd(t, HG_CHUNK))
    o_hg_p, hg_state_p = _hgrn(zp3, lb_all[l], hg_norm[l], s0_p, chunk, chunk)
    tm_m = 512 if t % 512 == 0 else t
    y1_p, h2_p, ti_p, tw_p = _merge(x_prompt.reshape(bp * t, d), o_nsa_p.reshape(bp * t, 512), o_hg_p.reshape(bp * t, 512),
                                    zp, g1[p_], sc2[p_], sh2[p_], wa, wb, wo, norm_mix_post[l], norm_ffn_pre[l],
                                    w_router[l], b_router[l], t, tm_m)

    xs = jnp.pad(x_sample, ((0, 0), (0, SUB - ts), (0, 0))).reshape(bs * SUB, d)
    rep = lambda m: jnp.broadcast_to(m[s_], (bs, SUB, d)).reshape(bs * SUB, d)
    tm_s = 512 if (bs * SUB) % 512 == 0 else bs * SUB
    zs = _inproj_rows(xs, norm_mix_pre[l], rep(sc1), rep(sh1), w_packed, tm_s)
    zs3 = zs.reshape(bs, SUB, ZW)
    win = state_win[l].reshape(bs, -1, 256)
    o_nsa_s = _nsa_sample(zs3, win, cache_cmp[l].reshape(n_pool, PAGE, 2, 128), cache_sel[l].reshape(n_pool, PAGE, 256),
                          page_table, w1t, pe_t, w2c,
                          _overlap_table(past // CMP_STRIDE, past // CMP_STRIDE - 1, 256),
                          _expand_table(128, past), ts)
    o_hg_s, hg_state_s = _hgrn(zs3, lb_all[l], hg_norm[l], state_hgrn[l], SUB, ts)
    y1_s, h2_s, ti_s, tw_s = _merge(xs, o_nsa_s.reshape(bs * SUB, 512), o_hg_s.reshape(bs * SUB, 512), zs,
                                    rep(g1), rep(sc2), rep(sh2), wa, wb, wo, norm_mix_post[l], norm_ffn_pre[l],
                                    w_router[l], b_router[l], 0, tm_s)
    real = lambda a: a.reshape(bs, SUB, -1)[:, :ts].reshape(bs * ts, -1)

    y1 = jnp.concatenate([y1_p, real(y1_s)], axis=0)
    h2 = jnp.concatenate([h2_p, real(h2_s)], axis=0)
    top_i = jnp.concatenate([ti_p, real(ti_s)], axis=0)[:, :TOP_K]
    top_w = jnp.concatenate([tw_p, real(tw_s)], axis=0)[:, :TOP_K]
    n_tok = y1.shape[0]
    tm_e = 512
    tile_expert, n_used, row_token, row_weight, pos_of = _route_tables(top_i, top_w, tm_e)
    y_sorted = _moe_ffn(h2, tile_expert, n_used, row_token, row_weight, wu, b_up[l], wd, b_down[l], tm_e)
    g2_rows = jnp.concatenate([jnp.broadcast_to(g2[p_], (bp, t, d)).reshape(bp * t, d),
                               jnp.broadcast_to(g2[s_], (bs, ts, d)).reshape(bs * ts, d)], axis=0)
    tm_c = 256 if n_tok % 256 == 0 else 8
    y = _combine(pos_of, y_sorted, y1, g2_rows, norm_ffn_post[l], tm_c)

    y_p = y[:bp * t].reshape(bp, t, d)
    y_s = y[bp * t:].reshape(bs, ts, d)
    keep = min(WINDOW, t)
    new_rows = zs3[:, :ts, C_KW:C_KW + 256].reshape(bs, ts, 2, NSA_KV, DH)
    win_s = jnp.concatenate([state_win[l], new_rows], axis=1)[:, ts:][None]
    return (y_p, y_s,
            _kv_out(zp3, C_KC, slice(0, t)), _kv_out(zs3, C_KC, slice(0, ts)),
            _kv_out(zp3, C_KS, slice(0, t)), _kv_out(zs3, C_KS, slice(0, ts)),
            _kv_out(zp3, C_KW, slice(t - keep, t)), win_s,
            hg_state_p[None], hg_state_s[None])
```

```python
import functools

import numpy as np
import jax
import jax.numpy as jnp
from jax import lax
from jax.experimental import pallas as pl
from jax.experimental.pallas import tpu as pltpu

F32, BF16, I32 = jnp.float32, jnp.bfloat16, jnp.int32

NSA_HEADS, NSA_KV, NSA_GROUP, DH = 8, 2, 4, 64
CMP_BLOCK, CMP_STRIDE, CMP_HIDDEN = 32, 16, 128
SLC_BLOCK, SLC_TOPK, WINDOW = 64, 16, 512
HG_HEADS, HG_DK, HG_DV, HG_CHUNK = 4, 128, 128, 64
N_EXPERTS, TOP_K = 32, 4
SWIGLU_LIMIT, SWIGLU_ALPHA = 7.0, 1.702
EPS, NEG, FORCE = 1e-6, -1e30, 1e4
PAGE = 128

C_GA, C_GB, C_Q, C_KC, C_KS, C_KW, C_GN = 0, 1024, 2048, 2560, 2816, 3072, 3328
C_HQ, C_HF, C_HI, C_HG, ZW = 3456, 3968, 4480, 4992, 5632
SUB = 8

VMEM_LIMIT = 56 * 1024 * 1024


def _cp(sem, vmem=VMEM_LIMIT):
    return pltpu.CompilerParams(dimension_semantics=sem, vmem_limit_bytes=vmem)


def _dot(a, b):
    return jnp.dot(a, b, preferred_element_type=F32)


def _dot_nt(a, b):
    return lax.dot_general(a, b, (((1,), (1,)), ((), ())), preferred_element_type=F32)


def _dot_tn(a, b):
    return lax.dot_general(a, b, (((0,), (0,)), ((), ())), preferred_element_type=F32)


def _split3(x):
    a = x.astype(BF16)
    r = x - a.astype(F32)
    b = r.astype(BF16)
    c = (r - b.astype(F32)).astype(BF16)
    return a, b, c


def _dot3(x, w_bf16):
    a, b, c = _split3(x)
    return _dot(a, w_bf16) + _dot(b, w_bf16) + _dot(c, w_bf16)


def _sigmoid(x):
    return 1.0 / (1.0 + jnp.exp(-x))


def _silu(x):
    return x * _sigmoid(x)


def _rms(x, g):
    return x * lax.rsqrt(jnp.mean(x * x, axis=-1, keepdims=True) + EPS) * g


def _gelu_tanh(x):
    return 0.5 * x * (1.0 + jnp.tanh(0.7978845608028654 * (x + 0.044715 * (x * x * x))))


def _mod_kernel(c_ref, w_ref, b_ref, o_ref):
    o_ref[...] = _dot(_silu(c_ref[...]).astype(BF16), w_ref[...].astype(BF16)) + b_ref[...]


def _modulation(c, w_ada, b_ada):
    m, d = c.shape
    nw = w_ada.shape[1]
    tn = nw // 6
    return pl.pallas_call(
        _mod_kernel,
        grid=(nw // tn,),
        in_specs=[pl.BlockSpec((m, d), lambda j: (0, 0)),
                  pl.BlockSpec((d, tn), lambda j: (0, j)),
                  pl.BlockSpec((1, tn), lambda j: (0, j))],
        out_specs=pl.BlockSpec((m, tn), lambda j: (0, j)),
        out_shape=jax.ShapeDtypeStruct((m, nw), F32),
        compiler_params=_cp(("arbitrary",)),
        name="modulation",
    )(c, w_ada, b_ada.reshape(1, nw))


def _inproj_kernel(x_ref, g_ref, sc_ref, sh_ref, w_ref, o_ref, h_scr):
    @pl.when(pl.program_id(1) == 0)
    def _():
        h = _rms(x_ref[...], g_ref[...]) * (1.0 + sc_ref[...]) + sh_ref[...]
        h_scr[...] = h.astype(BF16)

    o_ref[...] = _dot(h_scr[...], w_ref[...])


def _inproj(x2d, g, sc, sh, w_packed, rows_per_mod, tm, tn=512):
    n, d = x2d.shape
    zw = w_packed.shape[1]
    per = rows_per_mod // tm
    mod_spec = pl.BlockSpec((None, 1, d), lambda i, j: (i // per, 0, 0))
    return pl.pallas_call(
        _inproj_kernel,
        grid=(n // tm, zw // tn),
        in_specs=[pl.BlockSpec((tm, d), lambda i, j: (i, 0)),
                  pl.BlockSpec((1, d), lambda i, j: (0, 0)),
                  mod_spec, mod_spec,
                  pl.BlockSpec((d, tn), lambda i, j: (0, j))],
        out_specs=pl.BlockSpec((tm, tn), lambda i, j: (i, j)),
        out_shape=jax.ShapeDtypeStruct((n, zw), F32),
        scratch_shapes=[pltpu.VMEM((tm, d), BF16)],
        compiler_params=_cp(("arbitrary", "arbitrary")),
        name="inproj",
    )(x2d, g.reshape(1, d), sc, sh, w_packed)


def _inproj_rows(x2d, g, sc_rows, sh_rows, w_packed, tm, tn=512):
    n, d = x2d.shape
    zw = w_packed.shape[1]
    return pl.pallas_call(
        _inproj_kernel,
        grid=(n // tm, zw // tn),
        in_specs=[pl.BlockSpec((tm, d), lambda i, j: (i, 0)),
                  pl.BlockSpec((1, d), lambda i, j: (0, 0)),
                  pl.BlockSpec((tm, d), lambda i, j: (i, 0)),
                  pl.BlockSpec((tm, d), lambda i, j: (i, 0)),
                  pl.BlockSpec((d, tn), lambda i, j: (0, j))],
        out_specs=pl.BlockSpec((tm, tn), lambda i, j: (i, j)),
        out_shape=jax.ShapeDtypeStruct((n, zw), F32),
        scratch_shapes=[pltpu.VMEM((tm, d), BF16)],
        compiler_params=_cp(("arbitrary", "arbitrary")),
        name="inproj_rows",
    )(x2d, g.reshape(1, d), sc_rows, sh_rows, w_packed)


def _compress(load, nch, w1_ref, pe_ref, w2_ref):
    out = []
    for kv in range(2):
        w1 = w1_ref[kv]
        x = jnp.concatenate([load(kv, tok) for tok in range(CMP_STRIDE)], axis=1).astype(BF16)
        acc = _dot(x, w1)
        pe = _dot(pe_ref[kv].astype(BF16), w1)
        pre0 = pe[0:1, :CMP_HIDDEN] + pe[1:2, CMP_HIDDEN:2 * CMP_HIDDEN]
        w2 = w2_ref[kv].astype(BF16)
        row = []
        for g in range(2):
            c0 = g * 2 * CMP_HIDDEN
            pre = (acc[:, c0:c0 + CMP_HIDDEN]
                   + pltpu.roll(acc[:, c0 + CMP_HIDDEN:c0 + 2 * CMP_HIDDEN], nch - 1, 0) + pre0)
            row.append(_dot(_gelu_tanh(pre).astype(BF16), w2))
        out.append(row)
    return out


def _compress_prompt_kernel(k_ref, v_ref, w1_ref, pe_ref, w2_ref, o_ref, *, nch):
    def load(kv, tok):
        return (k_ref, v_ref)[kv][pl.ds(tok, nch, stride=CMP_STRIDE), :]

    out = _compress(load, nch, w1_ref, pe_ref, w2_ref)
    for kv in range(2):
        for g in range(2):
            o_ref[kv, g] = out[kv][g]


def _compress_prompt(z3, w1t, pe_t, w2):
    b, t, _ = z3.shape
    nch = t // CMP_STRIDE
    return pl.pallas_call(
        functools.partial(_compress_prompt_kernel, nch=nch),
        grid=(b,),
        in_specs=[pl.BlockSpec((None, t, 128), lambda i: (i, 0, C_KC // 128)),
                  pl.BlockSpec((None, t, 128), lambda i: (i, 0, C_KC // 128 + 1)),
                  pl.BlockSpec(w1t.shape, lambda i: (0, 0, 0)),
                  pl.BlockSpec(pe_t.shape, lambda i: (0, 0, 0)),
                  pl.BlockSpec(w2.shape, lambda i: (0, 0, 0))],
        out_specs=pl.BlockSpec((None, 2, 2, nch, DH), lambda i: (i, 0, 0, 0, 0)),
        out_shape=jax.ShapeDtypeStruct((b, 2, 2, nch, DH), F32),
        compiler_params=_cp(("arbitrary",)),
        name="compress_prompt",
    )(z3, z3, w1t, pe_t, w2)


def _slope(g, r):
    return 2.0 ** (-(g * NSA_GROUP + r + 1))


def _stack_heads(q, g):
    parts = [q[:, (g * NSA_GROUP + r) * DH:(g * NSA_GROUP + r + 1) * DH] for r in range(NSA_GROUP)]
    return (jnp.concatenate(parts, axis=0) * (DH ** -0.5)).astype(BF16)


def _head_cols(tq, q0, g):
    pos1 = q0 + lax.broadcasted_iota(I32, (tq, 1), 0)
    qpos = jnp.concatenate([pos1] * NSA_GROUP, axis=0)
    slope = jnp.concatenate([jnp.full((tq, 1), _slope(g, r), F32) for r in range(NSA_GROUP)], axis=0)
    return qpos, -slope * qpos.astype(F32), slope


def _bias_mask(s, kpos_row, valid, a_col, s_col):
    return jnp.where(valid, s + (a_col + s_col * kpos_row.astype(F32)), NEG)


def _masked_scores(q4, k_bf16, kpos_row, valid, a_col, s_col):
    return _bias_mask(_dot_nt(q4, k_bf16), kpos_row, valid, a_col, s_col)


def _softmax_two(s_p, s_n, vt_p, v_n):
    m = jnp.maximum(jnp.max(s_p, axis=-1, keepdims=True), jnp.max(s_n, axis=-1, keepdims=True))
    p_p = jnp.exp(s_p - m)
    p_n = jnp.exp(s_n - m)
    l = jnp.sum(p_p, axis=-1, keepdims=True) + jnp.sum(p_n, axis=-1, keepdims=True)
    return (_dot_nt(p_p.astype(BF16), vt_p) + _dot(p_n.astype(BF16), v_n)) / l


def _topk_mask(score, k):
    lane = lax.broadcasted_iota(I32, score.shape, 1).astype(F32)
    sel = jnp.zeros(score.shape, F32)
    for _ in range(k):
        mx = jnp.max(score, axis=-1, keepdims=True)
        idx = jnp.min(jnp.where(score == mx, lane, 1e9), axis=-1, keepdims=True)
        pick = lane == idx
        sel = jnp.where(pick, 1.0, sel)
        score = jnp.where(pick, -jnp.inf, score)
    return sel


def _cmp_branch(q4, kcb, vcb, ov_ref, qpos, a_col, s_col, tq, q0, n_slc):
    ncmp = kcb.shape[0]
    endpos = lax.broadcasted_iota(I32, (1, ncmp), 1) * CMP_STRIDE + (CMP_BLOCK - 1)
    valid = endpos <= qpos
    s = _masked_scores(q4, kcb.astype(BF16), endpos, valid, a_col, s_col)
    m = jnp.max(s, axis=-1, keepdims=True)
    e = jnp.exp(s - m)
    p = jnp.where(valid, e, 0.0) / jnp.sum(e, axis=-1, keepdims=True)
    o = _dot(p.astype(BF16), vcb.astype(BF16))
    psum = p[0:tq] + p[tq:2 * tq] + p[2 * tq:3 * tq] + p[3 * tq:4 * tq]
    imp = _dot3(psum, ov_ref[...])
    jl = lax.broadcasted_iota(I32, imp.shape, 1)
    cur = lax.shift_right_logical(q0 + lax.broadcasted_iota(I32, (tq, 1), 0), 6)
    forced = (jl == 0) | (jl == cur) | (jl == cur - 1)
    score = jnp.where(forced, FORCE, jnp.where(jl <= cur, imp, -FORCE))
    score = jnp.where(jl < n_slc, score, -3e38)
    return o, _topk_mask(score, min(SLC_TOPK, n_slc))


def _gate_combine(gn, outs, tq, g):
    sig = _sigmoid(gn)
    heads = []
    for r in range(NSA_GROUP):
        acc = None
        for br, o in enumerate(outs):
            c = br * NSA_HEADS + g * NSA_GROUP + r
            t = sig[:, c:c + 1] * o[r * tq:(r + 1) * tq, :]
            acc = t if acc is None else acc + t
        heads.append(acc)
    return jnp.concatenate(heads, axis=1)


def _nsa_prompt_kernel(q_ref, gn_ref, kvc_ref, ks_ref, vs_ref, kw_ref, vw_ref, ov_ref, e_ref, o_ref,
                       *, tq, tk, t_len, n_slc):
    i = pl.program_id(1)
    q0 = i * tq
    q = q_ref[...]
    gn = gn_ref[...]
    span = min(WINDOW + tq, t_len)
    for g in range(NSA_KV):
        lanes = slice(g * DH, (g + 1) * DH)
        q4 = _stack_heads(q, g)
        qpos, a_col, s_col = _head_cols(tq, q0, g)
        o_cmp, sel = _cmp_branch(q4, kvc_ref[0, g], kvc_ref[1, g], ov_ref, qpos, a_col, s_col, tq, q0, n_slc)
        sel4 = jnp.concatenate([sel] * NSA_GROUP, axis=0).astype(BF16)

        def body(kt, carry):
            m, l, acc = carry
            k0 = pl.multiple_of(kt * tk, tk)
            kpos = k0 + lax.broadcasted_iota(I32, (1, tk), 1)
            picked = _dot(sel4, e_ref[:, pl.ds(k0, tk)]) > 0.5
            valid = picked & (kpos <= qpos)
            s = _masked_scores(q4, ks_ref[pl.ds(k0, tk), lanes].astype(BF16), kpos, valid, a_col, s_col)
            m_new = jnp.maximum(m, jnp.max(s, axis=-1, keepdims=True))
            alpha = jnp.exp(m - m_new)
            p = jnp.exp(s - m_new)
            l = alpha * l + jnp.sum(p, axis=-1, keepdims=True)
            acc = alpha * acc + _dot(p.astype(BF16), vs_ref[pl.ds(k0, tk), lanes].astype(BF16))
            return m_new, l, acc

        init = (jnp.full((4 * tq, 1), NEG, F32), jnp.zeros((4 * tq, 1), F32), jnp.zeros((4 * tq, DH), F32))
        _, l, acc = lax.fori_loop(0, (q0 + tq + tk - 1) // tk, body, init)
        o_slc = acc / l

        w0 = pl.multiple_of(jnp.maximum(q0 + tq - span, 0), 128)
        kpos = w0 + lax.broadcasted_iota(I32, (1, span), 1)
        valid = (kpos <= qpos) & (kpos > qpos - WINDOW)
        s = _masked_scores(q4, kw_ref[pl.ds(w0, span), lanes].astype(BF16), kpos, valid, a_col, s_col)
        p = jnp.exp(s - jnp.max(s, axis=-1, keepdims=True))
        o_win = _dot(p.astype(BF16), vw_ref[pl.ds(w0, span), lanes].astype(BF16)) / jnp.sum(p, axis=-1, keepdims=True)

        o_ref[:, g * 256:(g + 1) * 256] = _gate_combine(gn, (o_cmp, o_slc, o_win), tq, g)


def _nsa_prompt(z3, kvcb, ov, e_mat, tq, tk):
    b, t, _ = z3.shape
    n_slc = -(-t // SLC_BLOCK)
    nq = t // tq
    kv_spec = lambda col: pl.BlockSpec((None, t, 128), lambda bi, i, c=col // 128: (bi, 0, c))
    return pl.pallas_call(
        functools.partial(_nsa_prompt_kernel, tq=tq, tk=tk, t_len=t, n_slc=n_slc),
        grid=(b, nq),
        in_specs=[pl.BlockSpec((None, tq, 512), lambda bi, i: (bi, i, C_Q // 512)),
                  pl.BlockSpec((None, tq, 128), lambda bi, i: (bi, i, C_GN // 128)),
                  pl.BlockSpec((None,) + kvcb.shape[1:], lambda bi, i: (bi, 0, 0, 0, 0)),
                  kv_spec(C_KS), kv_spec(C_KS + 128), kv_spec(C_KW), kv_spec(C_KW + 128),
                  pl.BlockSpec(ov.shape, lambda bi, i: (0, 0)),
                  pl.BlockSpec(e_mat.shape, lambda bi, i: (0, 0))],
        out_specs=pl.BlockSpec((None, tq, 512), lambda bi, i: (bi, i, 0)),
        out_shape=jax.ShapeDtypeStruct((b, t, 512), F32),
        compiler_params=_cp(("arbitrary", "arbitrary")),
        name="nsa_prompt",
    )(z3, z3, kvcb, z3, z3, z3, z3, ov, e_mat)


def _nsa_sample_kernel(pt_ref, q_ref, gn_ref, kvs_ref, kvw_ref, win_ref, cmp_hbm, sel_hbm,
                       w1_ref, pe_ref, w2_ref, ov_ref, e_ref, o_ref,
                       cmp_buf, sel_buf, rows_scr, new_scr, sem, *, n_pages, past, t_new):
    b = pl.program_id(0)
    nb = pl.num_programs(0)
    slot = b % 2
    nch = past // CMP_STRIDE
    wb = win_ref.shape[-1]

    def page_copies(bb, sl, pg):
        toks = pl.ds(pg * PAGE, PAGE)
        p = pt_ref[bb, pg]
        return (pltpu.make_async_copy(cmp_hbm.at[p], cmp_buf.at[sl, :, :, toks], sem.at[0, sl]),
                pltpu.make_async_copy(sel_hbm.at[p], sel_buf.at[sl, :, :, toks], sem.at[1, sl]))

    def fetch(bb, sl):
        for pg in range(n_pages):
            for c in page_copies(bb, sl, pg):
                c.start()

    @pl.when(b == 0)
    def _():
        fetch(0, 0)

    @pl.when(b + 1 < nb)
    def _():
        fetch(b + 1, 1 - slot)

    for pg in range(n_pages):
        for c in page_copies(b, slot, pg):
            c.wait()

    new_scr[...] = jnp.zeros(new_scr.shape, F32)
    new_scr[0:SUB, 0:256] = kvs_ref[...]
    new_scr[0:SUB, 256:512] = kvw_ref[...]

    tblk = 512 if past % 512 == 0 else past

    def load(kv, tok):
        if tok == 0:
            for c0 in range(0, past, tblk):
                rows_scr[c0:c0 + tblk, :] = cmp_buf[slot, kv, :, c0:c0 + tblk].T
        return rows_scr[pl.ds(tok, nch, stride=CMP_STRIDE), :]

    kvcb = _compress(load, nch, w1_ref, pe_ref, w2_ref)

    tq = SUB
    n_slc = -(-(past + t_new) // SLC_BLOCK)
    q = q_ref[...]
    gn = gn_ref[...]
    npos = past + lax.broadcasted_iota(I32, (1, 128), 1)
    ppos = lax.broadcasted_iota(I32, (1, past), 1)
    wpos = past - wb + lax.broadcasted_iota(I32, (1, wb), 1)
    for g in range(NSA_KV):
        rows = slice(g * DH, (g + 1) * DH)
        q4 = _stack_heads(q, g)
        qpos, a_col, s_col = _head_cols(tq, past, g)
        o_cmp, sel = _cmp_branch(q4, kvcb[0][g], kvcb[1][g], ov_ref, qpos, a_col, s_col, tq, past, n_slc)
        sel4 = jnp.concatenate([sel[:, :128]] * NSA_GROUP, axis=0).astype(BF16)

        picked = _dot(sel4, e_ref[...]) > 0.5
        s_p = _bias_mask(_dot(q4, sel_buf[slot, 0, rows, :].astype(BF16)), ppos, picked & (ppos <= qpos), a_col, s_col)
        s_n = _masked_scores(q4, new_scr[:, rows].astype(BF16), npos, npos <= qpos, a_col, s_col)
        o_slc = _softmax_two(s_p, s_n, sel_buf[slot, 1, rows, :].astype(BF16),
                             new_scr[:, 128 + g * DH:128 + (g + 1) * DH].astype(BF16))

        s_p = _bias_mask(_dot(q4, win_ref[0, rows, :].astype(BF16)), wpos, (wpos <= qpos) & (wpos > qpos - WINDOW),
                         a_col, s_col)
        s_n = _masked_scores(q4, new_scr[:, 256 + g * DH:256 + (g + 1) * DH].astype(BF16), npos, npos <= qpos,
                             a_col, s_col)
        o_win = _softmax_two(s_p, s_n, win_ref[1, rows, :].astype(BF16),
                             new_scr[:, 384 + g * DH:384 + (g + 1) * DH].astype(BF16))

        o_ref[:, g * 256:(g + 1) * 256] = _gate_combine(gn, (o_cmp, o_slc, o_win), tq, g)


def _nsa_sample(z3, win, cache_cmp, cache_sel, page_table, w1t, pe_t, w2, ov, e_mat, t_new):
    bs = z3.shape[0]
    n_pages = page_table.shape[1]
    past = n_pages * PAGE
    wb = win.shape[-1]
    const = lambda shape: pl.BlockSpec(shape, lambda bi, pt, n=len(shape): (0,) * n)
    grid_spec = pltpu.PrefetchScalarGridSpec(
        num_scalar_prefetch=1,
        grid=(bs,),
        in_specs=[pl.BlockSpec((None, SUB, 512), lambda bi, pt: (bi, 0, C_Q // 512)),
                  pl.BlockSpec((None, SUB, 128), lambda bi, pt: (bi, 0, C_GN // 128)),
                  pl.BlockSpec((None, SUB, 256), lambda bi, pt: (bi, 0, C_KS // 256)),
                  pl.BlockSpec((None, SUB, 256), lambda bi, pt: (bi, 0, C_KW // 256)),
                  pl.BlockSpec((None, 2, 128, wb), lambda bi, pt: (bi, 0, 0, 0)),
                  pl.BlockSpec(memory_space=pl.ANY),
                  pl.BlockSpec(memory_space=pl.ANY),
                  const(w1t.shape), const(pe_t.shape), const(w2.shape), const(ov.shape), const(e_mat.shape)],
        out_specs=pl.BlockSpec((None, SUB, 512), lambda bi, pt: (bi, 0, 0)),
        scratch_shapes=[pltpu.VMEM((2, 2, 128, past), F32),
                        pltpu.VMEM((2, 2, 128, past), F32),
                        pltpu.VMEM((past, 128), F32),
                        pltpu.VMEM((128, 512), F32),
                        pltpu.SemaphoreType.DMA((2, 2))],
    )
    return pl.pallas_call(
        functools.partial(_nsa_sample_kernel, n_pages=n_pages, past=past, t_new=t_new),
        grid_spec=grid_spec,
        out_shape=jax.ShapeDtypeStruct((bs, SUB, 512), F32),
        compiler_params=_cp(("arbitrary",), 60 * 1024 * 1024),
        name="nsa_sample",
    )(page_table, z3, z3, z3, z3, win, cache_cmp, cache_sel, w1t, pe_t, w2, ov, e_mat)


def _hgrn_kernel(hq_ref, hf_ref, hi_ref, hg_ref, lb_ref, nw_ref, s0_ref, o_ref, sout_ref, st_scr, o_scr,
                 *, t_len, chunk, t_real):
    st_scr[...] = s0_ref[...].T
    lb = lb_ref[...]
    nw = nw_ref[...]
    row = lax.broadcasted_iota(I32, (chunk, 1), 0)
    real = row < t_real
    tri = (lax.broadcasted_iota(I32, (chunk, chunk), 0) >= lax.broadcasted_iota(I32, (chunk, chunk), 1)).astype(BF16)
    blk = 8

    def step(ci, carry):
        rows = pl.ds(pl.multiple_of(ci * chunk, chunk), chunk)
        q = _silu(hq_ref[rows, :])
        f = lb + (1.0 - lb) * _sigmoid(hf_ref[rows, :])
        k = jnp.where(real, 1.0 - f, 0.0)
        gl = jnp.where(real, jnp.log(f), 0.0)
        v = hi_ref[rows, :]
        if chunk > blk:
            ga_, gb_, gc_ = _split3(gl)
            bcum = _dot(tri, ga_) + _dot(tri, gb_) + _dot(tri, gc_)
        else:
            bcum = jnp.zeros_like(gl)
            for s in range(t_real):
                bcum = bcum + jnp.where(row >= s, gl[s:s + 1, :], 0.0)
        st = st_scr[...]
        o_scr[...] = _dot_nt((q * jnp.exp(bcum)).astype(BF16), st.astype(BF16))
        for lo in range(0, min(chunk, t_real), blk):
            qj = q[lo:, :]
            bj = bcum[lo:, :]
            tj = row[lo:, :]
            acc = jnp.zeros((chunk - lo, HG_DV), F32)
            for s in range(lo, min(lo + blk, t_real)):
                d = jnp.where(tj >= s, bj - bcum[s:s + 1, :], NEG)
                w = jnp.sum(qj * jnp.exp(d) * k[s:s + 1, :], axis=-1, keepdims=True)
                acc = acc + w * v[s:s + 1, :]
            o_scr[lo:, :] = o_scr[lo:, :] + acc
        bc = bcum[chunk - 1:chunk, :]
        kt = k * jnp.exp(bc - bcum)
        st_scr[...] = st * jnp.exp(bc) + _dot_tn(v.astype(BF16), kt.astype(BF16))
        o = o_scr[...]
        o = o * lax.rsqrt(jnp.mean(o * o, axis=-1, keepdims=True) + EPS) * nw
        o_ref[rows, :] = o * _silu(hg_ref[rows, :])
        return carry

    lax.fori_loop(0, t_len // chunk, step, 0)
    sout_ref[...] = st_scr[...].T


def _hgrn(z3, lb, norm_w, s0, chunk, t_real):
    b, t, _ = z3.shape
    col = lambda c0: pl.BlockSpec((None, t, 128), lambda bi, h, c=c0 // 128: (bi, 0, c + h))
    return pl.pallas_call(
        functools.partial(_hgrn_kernel, t_len=t, chunk=chunk, t_real=t_real),
        grid=(b, HG_HEADS),
        in_specs=[col(C_HQ), col(C_HF), col(C_HI), col(C_HG),
                  pl.BlockSpec((None, 1, HG_DK), lambda bi, h: (h, 0, 0)),
                  pl.BlockSpec((1, HG_DV), lambda bi, h: (0, 0)),
                  pl.BlockSpec((None, None, HG_DK, HG_DV), lambda bi, h: (bi, h, 0, 0))],
        out_specs=[pl.BlockSpec((None, t, 128), lambda bi, h: (bi, 0, h)),
                   pl.BlockSpec((None, None, HG_DK, HG_DV), lambda bi, h: (bi, h, 0, 0))],
        out_shape=[jax.ShapeDtypeStruct((b, t, HG_HEADS * HG_DV), F32),
                   jax.ShapeDtypeStruct((b, HG_HEADS, HG_DK, HG_DV), F32)],
        scratch_shapes=[pltpu.VMEM((HG_DV, HG_DK), F32), pltpu.VMEM((chunk, HG_DV), F32)],
        compiler_params=_cp(("arbitrary", "arbitrary")),
        name="hgrn2",
    )(z3, z3, z3, z3, lb.reshape(HG_HEADS, 1, HG_DK), norm_w.reshape(1, HG_DV), s0)


def _merge_kernel(x_ref, on_ref, oh_ref, ga_ref, gb_ref, wa_ref, wb_ref, wo_ref, g1_ref, sc_ref, sh_ref,
                  npost_ref, npre_ref, wr_ref, br_ref, y_ref, h_ref, ti_ref, tw_ref):
    a = _dot(on_ref[...].astype(BF16), wa_ref[...])
    bb = _dot(oh_ref[...].astype(BF16), wb_ref[...])
    m = _sigmoid(ga_ref[...]) * a + _sigmoid(gb_ref[...]) * bb
    out = _dot(m.astype(BF16), wo_ref[...])
    y = x_ref[...] + g1_ref[...] * _rms(out, npost_ref[...])
    y_ref[...] = y
    h = _rms(y, npre_ref[...]) * (1.0 + sc_ref[...]) + sh_ref[...]
    h_ref[...] = h
    hh, hl, _ = _split3(h)
    w = wr_ref[...]
    wh = w.astype(BF16)
    wl = (w - wh.astype(F32)).astype(BF16)
    logits = _dot(hh, wh) + _dot(hh, wl) + _dot(hl, wh) + br_ref[...]
    lane = lax.broadcasted_iota(I32, logits.shape, 1)
    lanef = lane.astype(F32)
    logits = jnp.where(lane < N_EXPERTS, logits, -jnp.inf)
    ti = jnp.zeros(logits.shape, F32)
    tv = jnp.zeros(logits.shape, F32)
    v0 = None
    for kk in range(TOP_K):
        mx = jnp.max(logits, axis=-1, keepdims=True)
        idx = jnp.min(jnp.where(logits == mx, lanef, 1e9), axis=-1, keepdims=True)
        v0 = mx if v0 is None else v0
        ti = jnp.where(lane == kk, idx, ti)
        tv = jnp.where(lane == kk, jnp.exp(mx - v0), tv)
        logits = jnp.where(lanef == idx, -jnp.inf, logits)
    ti_ref[...] = ti.astype(I32)
    tw_ref[...] = tv / jnp.sum(tv, axis=-1, keepdims=True)


def _merge(x2d, o_nsa, o_hg, z2d, g1, sc2, sh2, wa, wb, wo, n_post, n_pre, w_router, b_router, rows_per_mod, tm):
    n, d = x2d.shape
    row = lambda w: pl.BlockSpec((tm, w), lambda i: (i, 0))
    if rows_per_mod:
        per = rows_per_mod // tm
        mod_spec = pl.BlockSpec((None, 1, d), lambda i: (i // per, 0, 0))
    else:
        mod_spec = row(d)
    const2 = lambda a: pl.BlockSpec(a.shape, lambda i: (0, 0))
    wr = jnp.pad(w_router, ((0, 0), (0, 128 - N_EXPERTS)))
    br = jnp.pad(b_router, (0, 128 - N_EXPERTS)).reshape(1, 128)
    vec = lambda v: v.reshape(1, d)
    return pl.pallas_call(
        _merge_kernel,
        grid=(n // tm,),
        in_specs=[row(d), row(512), row(512),
                  pl.BlockSpec((tm, d), lambda i: (i, C_GA // 1024)),
                  pl.BlockSpec((tm, d), lambda i: (i, C_GB // 1024)),
                  const2(wa), const2(wb), const2(wo),
                  mod_spec, mod_spec, mod_spec,
                  pl.BlockSpec((1, d), lambda i: (0, 0)), pl.BlockSpec((1, d), lambda i: (0, 0)),
                  const2(wr), const2(br)],
        out_specs=[row(d), row(d), row(128), row(128)],
        out_shape=[jax.ShapeDtypeStruct((n, d), F32), jax.ShapeDtypeStruct((n, d), F32),
                   jax.ShapeDtypeStruct((n, 128), I32), jax.ShapeDtypeStruct((n, 128), F32)],
        compiler_params=_cp(("arbitrary",)),
        name="merge_router",
    )(x2d, o_nsa, o_hg, z2d, z2d, wa, wb, wo, g1, sc2, sh2, vec(n_post), vec(n_pre), wr, br)


def _moe_kernel(wt_ref, we_ref, lo_ref, hi_ref, ni_ref, tok_ref, tok_next_ref, rw_ref, h_hbm,
                wu_ref, bu_ref, wd_ref, bd_ref, o_ref, xbuf, sem, *, tm, n_tiles):
    w = pl.program_id(0)
    tile = wt_ref[w]
    first = (w == 0) | (tile != wt_ref[jnp.maximum(w - 1, 0)])
    valid = w < ni_ref[0]
    slot = tile % 2

    def gather(idx_ref, sl):
        def body(r, c):
            pltpu.make_async_copy(h_hbm.at[pl.ds(idx_ref[r], 1), :], xbuf.at[sl, pl.ds(r, 1), :], sem.at[sl]).start()
            return c
        lax.fori_loop(0, tm, body, 0, unroll=8)

    @pl.when(valid & (w == 0))
    def _():
        gather(tok_ref, 0)

    @pl.when(valid & first & (tile + 1 < n_tiles))
    def _():
        gather(tok_next_ref, 1 - slot)

    @pl.when(valid & first)
    def _():
        pltpu.make_async_copy(h_hbm.at[pl.ds(0, tm), :], xbuf.at[slot], sem.at[slot]).wait()

    @pl.when(valid)
    def _():
        x = xbuf[slot].astype(BF16)
        z = _dot(x, wu_ref[...]) + bu_ref[...]
        de = wd_ref.shape[0]
        gate = jnp.minimum(z[:, :de], SWIGLU_LIMIT)
        up = jnp.clip(z[:, de:], -SWIGLU_LIMIT, SWIGLU_LIMIT)
        act = (up + 1.0) * gate * _sigmoid(SWIGLU_ALPHA * gate)
        y = _dot(act.astype(BF16), wd_ref[...]) + bd_ref[...]
        row = lax.broadcasted_iota(I32, (tm, 1), 0)
        mine = (row >= lo_ref[w]) & (row < hi_ref[w])
        contrib = jnp.where(mine, y * rw_ref[...], 0.0)

        @pl.when(first)
        def _():
            o_ref[...] = contrib

        @pl.when(jnp.logical_not(first))
        def _():
            o_ref[...] = o_ref[...] + contrib


def _moe_ffn(h2, items, row_token, row_weight, w_up, b_up, w_down, b_down, tm):
    n, d = h2.shape
    n_rows = row_token.shape[0]
    n_tiles = n_rows // tm
    ne, _, dh2 = w_up.shape
    wt, we, lo, hi, ni = items
    grid_spec = pltpu.PrefetchScalarGridSpec(
        num_scalar_prefetch=5,
        grid=(wt.shape[0],),
        in_specs=[pl.BlockSpec((tm,), lambda w, wt, *_: (wt[w],), memory_space=pltpu.SMEM),
                  pl.BlockSpec((tm,), lambda w, wt, *_: (jnp.minimum(wt[w] + 1, n_tiles - 1),), memory_space=pltpu.SMEM),
                  pl.BlockSpec((tm, 1), lambda w, wt, *_: (wt[w], 0)),
                  pl.BlockSpec(memory_space=pl.ANY),
                  pl.BlockSpec((None, d, dh2), lambda w, wt, we, *_: (we[w], 0, 0)),
                  pl.BlockSpec((None, 1, dh2), lambda w, wt, we, *_: (we[w], 0, 0)),
                  pl.BlockSpec((None, dh2 // 2, d), lambda w, wt, we, *_: (we[w], 0, 0)),
                  pl.BlockSpec((None, 1, d), lambda w, wt, we, *_: (we[w], 0, 0))],
        out_specs=pl.BlockSpec((tm, d), lambda w, wt, *_: (wt[w], 0)),
        scratch_shapes=[pltpu.VMEM((2, tm, d), F32), pltpu.SemaphoreType.DMA((2,))],
    )
    return pl.pallas_call(
        functools.partial(_moe_kernel, tm=tm, n_tiles=n_tiles),
        grid_spec=grid_spec,
        out_shape=jax.ShapeDtypeStruct((n_rows, d), F32),
        compiler_params=_cp(("arbitrary",)),
        name="moe_ffn",
    )(wt, we, lo, hi, ni, row_token, row_token, row_weight, h2, w_up, b_up.reshape(ne, 1, dh2),
      w_down, b_down.reshape(ne, 1, d))


def _combine_kernel(pos_ref, y_hbm, y1_ref, g2_ref, npost_ref, o_ref, buf, sem, *, tm):
    def body(r, c):
        for kk in range(TOP_K):
            pltpu.make_async_copy(y_hbm.at[pl.ds(pos_ref[r * TOP_K + kk], 1), :], buf.at[kk, pl.ds(r, 1), :], sem.at[0]).start()
        return c
    lax.fori_loop(0, tm, body, 0, unroll=4)
    for kk in range(TOP_K):
        pltpu.make_async_copy(y_hbm.at[pl.ds(0, tm), :], buf.at[kk], sem.at[0]).wait()
    moe = (buf[0] + buf[1]) + (buf[2] + buf[3])
    o_ref[...] = y1_ref[...] + g2_ref[...] * _rms(moe, npost_ref[...])


def _combine(pos_flat, y_sorted, y1, g2, n_post, rows_per_mod, tm):
    n, d = y1.shape
    if rows_per_mod:
        per = rows_per_mod // tm
        mod_spec = pl.BlockSpec((None, 1, d), lambda i: (i // per, 0, 0))
    else:
        mod_spec = pl.BlockSpec((tm, d), lambda i: (i, 0))
    return pl.pallas_call(
        functools.partial(_combine_kernel, tm=tm),
        grid=(n // tm,),
        in_specs=[pl.BlockSpec((tm * TOP_K,), lambda i: (i,), memory_space=pltpu.SMEM),
                  pl.BlockSpec(memory_space=pl.ANY),
                  pl.BlockSpec((tm, d), lambda i: (i, 0)),
                  mod_spec,
                  pl.BlockSpec((1, d), lambda i: (0, 0))],
        out_specs=pl.BlockSpec((tm, d), lambda i: (i, 0)),
        out_shape=jax.ShapeDtypeStruct((n, d), F32),
        scratch_shapes=[pltpu.VMEM((TOP_K, tm, d), F32), pltpu.SemaphoreType.DMA((1,))],
        compiler_params=_cp(("arbitrary",)),
        name="moe_combine",
    )(pos_flat, y_sorted, y1, g2, n_post.reshape(1, d))


def _route_tables(top_i, top_w, tm):
    m = top_i.shape[0] * TOP_K
    m_pad = -(-m // tm) * tm
    n_tiles = m_pad // tm
    e_flat = top_i.reshape(-1).astype(I32)
    iota = jnp.arange(m, dtype=I32)
    _, order, w_sorted = lax.sort((e_flat, iota, top_w.reshape(-1)), num_keys=1, is_stable=True)
    _, pos_of = lax.sort((order, iota), num_keys=1)
    row_token = jnp.pad(order // TOP_K, (0, m_pad - m))
    row_weight = jnp.pad(w_sorted, (0, m_pad - m)).reshape(m_pad, 1)
    ex = jnp.arange(N_EXPERTS, dtype=I32)
    counts = jnp.sum((e_flat[:, None] == ex[None, :]).astype(I32), axis=0)
    uend = jnp.cumsum(counts)
    ustart = uend - counts
    first_tile = ustart // tm
    n_item_e = jnp.where(counts > 0, (uend - 1) // tm - first_tile + 1, 0)
    iend = jnp.cumsum(n_item_e)
    n_items = iend[-1]
    wid = jnp.arange(n_tiles + N_EXPERTS - 1, dtype=I32)
    we = jnp.minimum(jnp.sum((wid[:, None] >= iend[None, :]).astype(I32), axis=1), N_EXPERTS - 1)
    onehot = (we[:, None] == ex[None, :]).astype(I32)
    pick = lambda v: jnp.sum(onehot * v[None, :], axis=1)
    wt = pick(first_tile) + (wid - pick(iend - n_item_e))
    lo = jnp.maximum(pick(ustart), wt * tm) - wt * tm
    hi = jnp.minimum(pick(uend), (wt + 1) * tm) - wt * tm
    live = wid < n_items
    last_e = jnp.max(jnp.where(counts > 0, ex, 0))
    items = (jnp.where(live, wt, n_tiles - 1), jnp.where(live, we, last_e),
             jnp.where(live, lo, 0), jnp.where(live, hi, 0), n_items.reshape(1))
    return tuple(a.astype(I32) for a in items), row_token, row_weight, pos_of


def _overlap_table(n_rows, n_cmp, n_cols):
    start = np.arange(n_rows)[:, None] * CMP_STRIDE
    j0 = np.arange(n_cols)[None, :] * SLC_BLOCK
    ov = (start < j0 + SLC_BLOCK) & (start + CMP_BLOCK > j0) & (np.arange(n_rows)[:, None] < n_cmp)
    return jnp.asarray(ov.astype(np.float32), dtype=BF16)


def _expand_table(n_rows, n_keys):
    e = (np.arange(n_keys)[None, :] // SLC_BLOCK) == np.arange(n_rows)[:, None]
    return jnp.asarray(e.astype(np.float32), dtype=BF16)


def _pack_w_in(w_in):
    d = w_in.shape[0]
    q = w_in[:, 0:512]
    kv = w_in[:, 512:1280]
    gn = w_in[:, 1280:1304]
    hh = w_in[:, 1304:3352]
    ga = w_in[:, 3352:4376]
    gb = w_in[:, 4376:5400]
    z = lambda w: jnp.zeros((d, w), w_in.dtype)
    return jnp.concatenate([ga, gb, q, kv, gn, z(128 - 24), hh, z(ZW - C_HG - 512)], axis=1).astype(BF16)


def _pack_compress(cmp_pe, cmp_w1, cmp_w2):
    r = CMP_BLOCK // CMP_STRIDE
    wt = cmp_w1.reshape(2, r, CMP_STRIDE, DH, CMP_HIDDEN).transpose(0, 2, 3, 1, 4).reshape(2, CMP_STRIDE, DH, r * CMP_HIDDEN)
    zero = jnp.zeros_like(wt)
    w1bd = jnp.concatenate([jnp.concatenate([wt, zero], axis=-1), jnp.concatenate([zero, wt], axis=-1)], axis=2)
    w1bd = w1bd.reshape(2, CMP_STRIDE * 2 * DH, 2 * r * CMP_HIDDEN)
    pe_t = jnp.pad(cmp_pe.reshape(2, r, CMP_STRIDE, DH), ((0, 0), (0, SUB - r), (0, 0), (0, DH)))
    return w1bd.astype(BF16), pe_t.reshape(2, SUB, CMP_STRIDE * 2 * DH), cmp_w2


def _kv_out(z3, c0, rows):
    b = z3.shape[0]
    return z3[:, rows, c0:c0 + 256].reshape(b, -1, 2, NSA_KV, DH)[None]


def kernel(x_prompt, x_sample, c_prompt, c_sample, cache_cmp, cache_sel, state_win, state_hgrn, page_table, w_ada, b_ada, norm_mix_pre, norm_mix_post, norm_ffn_pre, norm_ffn_post, w_in, cmp_pe, cmp_w1, cmp_w2, hg_lb_logits, hg_norm, w_branch_a, w_branch_b, w_out, w_router, b_router, w_up, b_up, w_down, b_down):
    bp, t, d = x_prompt.shape
    bs, ts, _ = x_sample.shape
    depth = w_in.shape[0]
    assert depth == 1 and ts <= SUB
    n_pool = cache_cmp.shape[1]
    past = page_table.shape[1] * PAGE
    lb_all = jnp.cumsum(jax.nn.softmax(hg_lb_logits.astype(F32), axis=0), axis=0)

    l = 0
    w_packed = _pack_w_in(w_in[l])
    w1t, pe_t, w2c = _pack_compress(cmp_pe[l], cmp_w1[l], cmp_w2[l])
    wa, wb, wo = w_branch_a[l].astype(BF16), w_branch_b[l].astype(BF16), w_out[l].astype(BF16)
    wu, wd = w_up[l].astype(BF16), w_down[l].astype(BF16)

    mod = _modulation(jnp.concatenate([c_prompt, c_sample], axis=0), w_ada[l], b_ada[l])
    mods = [m[:, None, :] for m in jnp.split(mod, 6, axis=-1)]
    sh1, sc1, g1, sh2, sc2, g2 = mods
    p_, s_ = slice(0, bp), slice(bp, bp + bs)

    tm_p = 1024 if t % 1024 == 0 else t
    zp = _inproj(x_prompt.reshape(bp * t, d), norm_mix_pre[l], sc1[p_], sh1[p_], w_packed, t, tm_p)
    zp3 = zp.reshape(bp, t, ZW)
    nch = t // CMP_STRIDE
    n_slc = -(-t // SLC_BLOCK)
    kvcb_p = _compress_prompt(zp3, w1t, pe_t, w2c)
    tq = 256 if t % 256 == 0 else t
    tk = 512 if t % 512 == 0 else t
    o_nsa_p = _nsa_prompt(zp3, kvcb_p, _overlap_table(nch, nch - 1, 128), _expand_table(128, t), tq, tk)
    s0_p = jnp.zeros((bp, HG_HEADS, HG_DK, HG_DV), F32)
    chunk = int(np.gcd(t, HG_CHUNK))
    o_hg_p, hg_state_p = _hgrn(zp3, lb_all[l], hg_norm[l], s0_p, chunk, chunk)
    tm_m = 512 if t % 512 == 0 else t
    y1_p, h2_p, ti_p, tw_p = _merge(x_prompt.reshape(bp * t, d), o_nsa_p.reshape(bp * t, 512), o_hg_p.reshape(bp * t, 512),
                                    zp, g1[p_], sc2[p_], sh2[p_], wa, wb, wo, norm_mix_post[l], norm_ffn_pre[l],
                                    w_router[l], b_router[l], t, tm_m)

    xs = jnp.pad(x_sample, ((0, 0), (0, SUB - ts), (0, 0))).reshape(bs * SUB, d)
    rep = lambda m: jnp.broadcast_to(m[s_], (bs, SUB, d)).reshape(bs * SUB, d)
    tm_s = 512 if (bs * SUB) % 512 == 0 else bs * SUB
    zs = _inproj_rows(xs, norm_mix_pre[l], rep(sc1), rep(sh1), w_packed, tm_s)
    zs3 = zs.reshape(bs, SUB, ZW)
    tok_minor = lambda a: jnp.transpose(a, (0, 2, 3, 4, 1)).reshape(a.shape[0], 2, NSA_KV * DH, a.shape[1])
    o_nsa_s = _nsa_sample(zs3, tok_minor(state_win[l]), tok_minor(cache_cmp[l]), tok_minor(cache_sel[l]),
                          page_table, w1t, pe_t, w2c,
                          _overlap_table(past // CMP_STRIDE, past // CMP_STRIDE - 1, 256),
                          _expand_table(128, past), ts)
    o_hg_s, hg_state_s = _hgrn(zs3, lb_all[l], hg_norm[l], state_hgrn[l], SUB, ts)
    y1_s, h2_s, ti_s, tw_s = _merge(xs, o_nsa_s.reshape(bs * SUB, 512), o_hg_s.reshape(bs * SUB, 512), zs,
                                    rep(g1), rep(sc2), rep(sh2), wa, wb, wo, norm_mix_post[l], norm_ffn_pre[l],
                                    w_router[l], b_router[l], 0, tm_s)
    real = lambda a: a.reshape(bs, SUB, -1)[:, :ts].reshape(bs * ts, -1)

    h2 = jnp.concatenate([h2_p, real(h2_s)], axis=0)
    top_i = jnp.concatenate([ti_p[:, :TOP_K], real(ti_s)[:, :TOP_K]], axis=0)
    top_w = jnp.concatenate([tw_p[:, :TOP_K], real(tw_s)[:, :TOP_K]], axis=0)
    tm_e = 512
    items, row_token, row_weight, pos_of = _route_tables(top_i, top_w, tm_e)
    y_sorted = _moe_ffn(h2, items, row_token, row_weight, wu, b_up[l], wd, b_down[l], tm_e)
    np_tok = bp * t
    tm_cp = 256 if t % 256 == 0 else t
    y_p = _combine(pos_of[:np_tok * TOP_K], y_sorted, y1_p, g2[p_], norm_ffn_post[l], t, tm_cp).reshape(bp, t, d)
    g2_s = jnp.broadcast_to(g2[s_], (bs, ts, d)).reshape(bs * ts, d)
    tm_cs = 256 if (bs * ts) % 256 == 0 else bs * ts
    y_s = _combine(pos_of[np_tok * TOP_K:], y_sorted, real(y1_s), g2_s, norm_ffn_post[l], 0, tm_cs).reshape(bs, ts, d)
    keep = min(WINDOW, t)
    new_rows = zs3[:, :ts, C_KW:C_KW + 256].reshape(bs, ts, 2, NSA_KV, DH)
    win_s = jnp.concatenate([state_win[l], new_rows], axis=1)[:, ts:][None]
    return (y_p, y_s,
            _kv_out(zp3, C_KC, slice(0, t)), _kv_out(zs3, C_KC, slice(0, ts)),
            _kv_out(zp3, C_KS, slice(0, t)), _kv_out(zs3, C_KS, slice(0, ts)),
            _kv_out(zp3, C_KW, slice(t - keep, t)), win_s,
            hg_state_p[None], hg_state_s[None])
```

```python
import functools

import numpy as np
import jax
import jax.numpy as jnp
from jax import lax
from jax.experimental import pallas as pl
from jax.experimental.pallas import tpu as pltpu

F32, BF16, I32 = jnp.float32, jnp.bfloat16, jnp.int32

NSA_HEADS, NSA_KV, NSA_GROUP, DH = 8, 2, 4, 64
CMP_BLOCK, CMP_STRIDE, CMP_HIDDEN = 32, 16, 128
SLC_BLOCK, SLC_TOPK, WINDOW = 64, 16, 512
HG_HEADS, HG_DK, HG_DV, HG_CHUNK = 4, 128, 128, 64
N_EXPERTS, TOP_K = 32, 4
SWIGLU_LIMIT, SWIGLU_ALPHA = 7.0, 1.702
EPS, NEG, FORCE = 1e-6, -1e30, 1e4
PAGE = 128

C_GA, C_GB, C_Q, C_HQ, C_HF, C_HI, C_HG = 0, 1024, 2048, 2560, 3072, 3584, 4096
C_KC, C_KS, C_KW, C_GN, ZW = 4608, 4864, 5120, 5376, 5632
SUB = 8

VMEM_LIMIT = 56 * 1024 * 1024


def _cp(sem, vmem=VMEM_LIMIT):
    return pltpu.CompilerParams(dimension_semantics=sem, vmem_limit_bytes=vmem)


def _dot(a, b):
    return jnp.dot(a, b, preferred_element_type=F32)


def _dot_nt(a, b):
    return lax.dot_general(a, b, (((1,), (1,)), ((), ())), preferred_element_type=F32)


def _dot_tn(a, b):
    return lax.dot_general(a, b, (((0,), (0,)), ((), ())), preferred_element_type=F32)


def _split3(x):
    a = x.astype(BF16)
    r = x - a.astype(F32)
    b = r.astype(BF16)
    c = (r - b.astype(F32)).astype(BF16)
    return a, b, c


def _dot3(x, w_bf16):
    a, b, c = _split3(x)
    return _dot(a, w_bf16) + _dot(b, w_bf16) + _dot(c, w_bf16)


def _sigmoid(x):
    return 1.0 / (1.0 + jnp.exp(-x))


def _silu(x):
    return x * _sigmoid(x)


def _rms(x, g):
    return x * lax.rsqrt(jnp.mean(x * x, axis=-1, keepdims=True) + EPS) * g


def _gelu_tanh(x):
    return 0.5 * x * (1.0 + jnp.tanh(0.7978845608028654 * (x + 0.044715 * (x * x * x))))


def _mod_kernel(c_ref, w_ref, b_ref, o_ref):
    o_ref[...] = _dot(_silu(c_ref[...]).astype(BF16), w_ref[...].astype(BF16)) + b_ref[...]


def _modulation(c, w_ada, b_ada):
    m, d = c.shape
    nw = w_ada.shape[1]
    tn = nw // 6
    return pl.pallas_call(
        _mod_kernel,
        grid=(nw // tn,),
        in_specs=[pl.BlockSpec((m, d), lambda j: (0, 0)),
                  pl.BlockSpec((d, tn), lambda j: (0, j)),
                  pl.BlockSpec((1, tn), lambda j: (0, j))],
        out_specs=pl.BlockSpec((m, tn), lambda j: (0, j)),
        out_shape=jax.ShapeDtypeStruct((m, nw), F32),
        compiler_params=_cp(("arbitrary",)),
        name="modulation",
    )(c, w_ada, b_ada.reshape(1, nw))


INPROJ_TN = 512


def _inproj_kernel(x_ref, g_ref, sc_ref, sh_ref, w_ref, o_ref):
    h = (_rms(x_ref[...], g_ref[...]) * (1.0 + sc_ref[...]) + sh_ref[...]).astype(BF16)
    for c0 in range(0, o_ref.shape[1], INPROJ_TN):
        o_ref[:, c0:c0 + INPROJ_TN] = _dot(h, w_ref[:, c0:c0 + INPROJ_TN])


def _inproj(x2d, g, sc, sh, w_packed, rows_per_mod, tm):
    n, d = x2d.shape
    zw = w_packed.shape[1]
    if rows_per_mod:
        per = rows_per_mod // tm
        mod_spec = pl.BlockSpec((None, 1, d), lambda i: (i // per, 0, 0))
    else:
        mod_spec = pl.BlockSpec((tm, d), lambda i: (i, 0))
    return pl.pallas_call(
        _inproj_kernel,
        grid=(n // tm,),
        in_specs=[pl.BlockSpec((tm, d), lambda i: (i, 0)),
                  pl.BlockSpec((1, d), lambda i: (0, 0)),
                  mod_spec, mod_spec,
                  pl.BlockSpec((d, zw), lambda i: (0, 0))],
        out_specs=pl.BlockSpec((tm, zw), lambda i: (i, 0)),
        out_shape=jax.ShapeDtypeStruct((n, zw), F32),
        compiler_params=_cp(("arbitrary",)),
        name="inproj",
    )(x2d, g.reshape(1, d), sc, sh, w_packed)


def _compress(load, nch, w1_ref, pe_ref, w2_ref):
    out = []
    for kv in range(2):
        w1 = w1_ref[kv]
        x = jnp.concatenate([load(kv, tok) for tok in range(CMP_STRIDE)], axis=1).astype(BF16)
        acc = _dot(x, w1)
        pe = _dot(pe_ref[kv].astype(BF16), w1)
        pre0 = pe[0:1, :CMP_HIDDEN] + pe[1:2, CMP_HIDDEN:2 * CMP_HIDDEN]
        w2 = w2_ref[kv].astype(BF16)
        row = []
        for g in range(2):
            c0 = g * 2 * CMP_HIDDEN
            pre = (acc[:, c0:c0 + CMP_HIDDEN]
                   + pltpu.roll(acc[:, c0 + CMP_HIDDEN:c0 + 2 * CMP_HIDDEN], nch - 1, 0) + pre0)
            row.append(_dot(_gelu_tanh(pre).astype(BF16), w2))
        out.append(row)
    return out


def _compress_prompt_kernel(k_ref, v_ref, w1_ref, pe_ref, w2_ref, o_ref, *, nch):
    def load(kv, tok):
        return (k_ref, v_ref)[kv][pl.ds(tok, nch, stride=CMP_STRIDE), :]

    out = _compress(load, nch, w1_ref, pe_ref, w2_ref)
    for kv in range(2):
        for g in range(2):
            o_ref[kv, g] = out[kv][g]


def _compress_prompt(z3, w1t, pe_t, w2):
    b, t, _ = z3.shape
    nch = t // CMP_STRIDE
    return pl.pallas_call(
        functools.partial(_compress_prompt_kernel, nch=nch),
        grid=(b,),
        in_specs=[pl.BlockSpec((None, t, 128), lambda i: (i, 0, C_KC // 128)),
                  pl.BlockSpec((None, t, 128), lambda i: (i, 0, C_KC // 128 + 1)),
                  pl.BlockSpec(w1t.shape, lambda i: (0, 0, 0)),
                  pl.BlockSpec(pe_t.shape, lambda i: (0, 0, 0)),
                  pl.BlockSpec(w2.shape, lambda i: (0, 0, 0))],
        out_specs=pl.BlockSpec((None, 2, 2, nch, DH), lambda i: (i, 0, 0, 0, 0)),
        out_shape=jax.ShapeDtypeStruct((b, 2, 2, nch, DH), F32),
        compiler_params=_cp(("arbitrary",)),
        name="compress_prompt",
    )(z3, z3, w1t, pe_t, w2)


def _slope(g, r):
    return 2.0 ** (-(g * NSA_GROUP + r + 1))


def _stack_heads(q, g):
    parts = [q[:, (g * NSA_GROUP + r) * DH:(g * NSA_GROUP + r + 1) * DH] for r in range(NSA_GROUP)]
    return (jnp.concatenate(parts, axis=0) * (DH ** -0.5)).astype(BF16)


def _head_cols(tq, q0, g):
    pos1 = q0 + lax.broadcasted_iota(I32, (tq, 1), 0)
    qpos = jnp.concatenate([pos1] * NSA_GROUP, axis=0)
    slope = jnp.concatenate([jnp.full((tq, 1), _slope(g, r), F32) for r in range(NSA_GROUP)], axis=0)
    return qpos, -slope * qpos.astype(F32), slope


def _bias_mask(s, kpos_row, valid, a_col, s_col):
    return jnp.where(valid, s + (a_col + s_col * kpos_row.astype(F32)), NEG)


def _masked_scores(q4, k_bf16, kpos_row, valid, a_col, s_col):
    return _bias_mask(_dot_nt(q4, k_bf16), kpos_row, valid, a_col, s_col)


def _softmax_two(s_p, s_n, vt_p, v_n):
    m = jnp.maximum(jnp.max(s_p, axis=-1, keepdims=True), jnp.max(s_n, axis=-1, keepdims=True))
    p_p = jnp.exp(s_p - m)
    p_n = jnp.exp(s_n - m)
    l = jnp.sum(p_p, axis=-1, keepdims=True) + jnp.sum(p_n, axis=-1, keepdims=True)
    return (_dot_nt(p_p.astype(BF16), vt_p) + _dot(p_n.astype(BF16), v_n)) / l


def _topk_mask(score, k, n):
    lane = lax.broadcasted_iota(I32, score.shape, 1)
    ahead = jnp.zeros(score.shape, F32)
    for i in range(n):
        col = score[:, i:i + 1]
        ahead = ahead + jnp.where((col > score) | ((col == score) & (lane > i)), 1.0, 0.0)
    return jnp.where((ahead < k) & (lane < n), 1.0, 0.0)


def _cmp_branch(q4, kcb, vcb, ov_ref, qpos, a_col, s_col, tq, q0, n_slc):
    ncmp = kcb.shape[0]
    endpos = lax.broadcasted_iota(I32, (1, ncmp), 1) * CMP_STRIDE + (CMP_BLOCK - 1)
    valid = endpos <= qpos
    s = _masked_scores(q4, kcb.astype(BF16), endpos, valid, a_col, s_col)
    m = jnp.max(s, axis=-1, keepdims=True)
    e = jnp.exp(s - m)
    p = jnp.where(valid, e, 0.0) / jnp.sum(e, axis=-1, keepdims=True)
    o = _dot(p.astype(BF16), vcb.astype(BF16))
    psum = p[0:tq] + p[tq:2 * tq] + p[2 * tq:3 * tq] + p[3 * tq:4 * tq]
    imp = _dot3(psum, ov_ref[...])
    jl = lax.broadcasted_iota(I32, imp.shape, 1)
    cur = lax.shift_right_logical(q0 + lax.broadcasted_iota(I32, (tq, 1), 0), 6)
    forced = (jl == 0) | (jl == cur) | (jl == cur - 1)
    score = jnp.where(forced, FORCE, jnp.where(jl <= cur, imp, -FORCE))
    return o, _topk_mask(score, min(SLC_TOPK, n_slc), n_slc)


def _gate_combine(gn, outs, tq, g):
    sig = _sigmoid(gn)
    heads = []
    for r in range(NSA_GROUP):
        acc = None
        for br, o in enumerate(outs):
            c = br * NSA_HEADS + g * NSA_GROUP + r
            t = sig[:, c:c + 1] * o[r * tq:(r + 1) * tq, :]
            acc = t if acc is None else acc + t
        heads.append(acc)
    return jnp.concatenate(heads, axis=1)


def _nsa_prompt_kernel(q_ref, gn_ref, kvc_ref, ks_ref, vs_ref, kw_ref, vw_ref, ov_ref, e_ref, o_ref,
                       *, tq, tk, t_len, n_slc):
    i = pl.program_id(1)
    q0 = i * tq
    q = q_ref[...]
    gn = gn_ref[...]
    span = min(WINDOW + tq, t_len)
    for g in range(NSA_KV):
        lanes = slice(g * DH, (g + 1) * DH)
        q4 = _stack_heads(q, g)
        qpos, a_col, s_col = _head_cols(tq, q0, g)
        o_cmp, sel = _cmp_branch(q4, kvc_ref[0, g], kvc_ref[1, g], ov_ref, qpos, a_col, s_col, tq, q0, n_slc)
        sel4 = jnp.concatenate([sel] * NSA_GROUP, axis=0).astype(BF16)

        def body(kt, carry):
            m, l, acc = carry
            k0 = pl.multiple_of(kt * tk, tk)
            kpos = k0 + lax.broadcasted_iota(I32, (1, tk), 1)
            picked = _dot(sel4, e_ref[:, pl.ds(k0, tk)]) > 0.5
            valid = picked & (kpos <= qpos)
            s = _masked_scores(q4, ks_ref[pl.ds(k0, tk), lanes].astype(BF16), kpos, valid, a_col, s_col)
            m_new = jnp.maximum(m, jnp.max(s, axis=-1, keepdims=True))
            alpha = jnp.exp(m - m_new)
            p = jnp.exp(s - m_new)
            l = alpha * l + jnp.sum(p, axis=-1, keepdims=True)
            acc = alpha * acc + _dot(p.astype(BF16), vs_ref[pl.ds(k0, tk), lanes].astype(BF16))
            return m_new, l, acc

        init = (jnp.full((4 * tq, 1), NEG, F32), jnp.zeros((4 * tq, 1), F32), jnp.zeros((4 * tq, DH), F32))
        _, l, acc = lax.fori_loop(0, (q0 + tq + tk - 1) // tk, body, init)
        o_slc = acc / l

        w0 = pl.multiple_of(jnp.maximum(q0 + tq - span, 0), 128)
        kpos = w0 + lax.broadcasted_iota(I32, (1, span), 1)
        valid = (kpos <= qpos) & (kpos > qpos - WINDOW)
        s = _masked_scores(q4, kw_ref[pl.ds(w0, span), lanes].astype(BF16), kpos, valid, a_col, s_col)
        p = jnp.exp(s - jnp.max(s, axis=-1, keepdims=True))
        o_win = _dot(p.astype(BF16), vw_ref[pl.ds(w0, span), lanes].astype(BF16)) / jnp.sum(p, axis=-1, keepdims=True)

        o_ref[:, g * 256:(g + 1) * 256] = _gate_combine(gn, (o_cmp, o_slc, o_win), tq, g)


def _nsa_prompt(z3, kvcb, ov, e_mat, tq, tk):
    b, t, _ = z3.shape
    n_slc = -(-t // SLC_BLOCK)
    nq = t // tq
    kv_spec = lambda col: pl.BlockSpec((None, t, 128), lambda bi, i, c=col // 128: (bi, 0, c))
    return pl.pallas_call(
        functools.partial(_nsa_prompt_kernel, tq=tq, tk=tk, t_len=t, n_slc=n_slc),
        grid=(b, nq),
        in_specs=[pl.BlockSpec((None, tq, 512), lambda bi, i: (bi, i, C_Q // 512)),
                  pl.BlockSpec((None, tq, 128), lambda bi, i: (bi, i, C_GN // 128)),
                  pl.BlockSpec((None,) + kvcb.shape[1:], lambda bi, i: (bi, 0, 0, 0, 0)),
                  kv_spec(C_KS), kv_spec(C_KS + 128), kv_spec(C_KW), kv_spec(C_KW + 128),
                  pl.BlockSpec(ov.shape, lambda bi, i: (0, 0)),
                  pl.BlockSpec(e_mat.shape, lambda bi, i: (0, 0))],
        out_specs=pl.BlockSpec((None, tq, 512), lambda bi, i: (bi, i, 0)),
        out_shape=jax.ShapeDtypeStruct((b, t, 512), F32),
        compiler_params=_cp(("arbitrary", "arbitrary")),
        name="nsa_prompt",
    )(z3, z3, kvcb, z3, z3, z3, z3, ov, e_mat)


def _nsa_sample_kernel(pt_ref, q_ref, gn_ref, kvs_ref, kvw_ref, win_ref, cmp_hbm, sel_hbm,
                       w1_ref, pe_ref, w2_ref, ov_ref, e_ref, o_ref,
                       cmp_buf, sel_buf, rows_scr, new_scr, sem, *, n_pages, past, t_new):
    b = pl.program_id(0)
    nb = pl.num_programs(0)
    slot = b % 2
    nch = past // CMP_STRIDE
    wb = win_ref.shape[-1]

    def page_copies(bb, sl, pg):
        toks = pl.ds(pg * PAGE, PAGE)
        p = pt_ref[bb, pg]
        return (pltpu.make_async_copy(cmp_hbm.at[p], cmp_buf.at[sl, :, :, toks], sem.at[0, sl]),
                pltpu.make_async_copy(sel_hbm.at[p], sel_buf.at[sl, :, :, toks], sem.at[1, sl]))

    def fetch(bb, sl):
        for pg in range(n_pages):
            for c in page_copies(bb, sl, pg):
                c.start()

    @pl.when(b == 0)
    def _():
        fetch(0, 0)

    @pl.when(b + 1 < nb)
    def _():
        fetch(b + 1, 1 - slot)

    for pg in range(n_pages):
        for c in page_copies(b, slot, pg):
            c.wait()

    new_scr[...] = jnp.zeros(new_scr.shape, F32)
    new_scr[0:SUB, 0:256] = kvs_ref[...]
    new_scr[0:SUB, 256:512] = kvw_ref[...]

    tblk = 512 if past % 512 == 0 else past

    def load(kv, tok):
        if tok == 0:
            for c0 in range(0, past, tblk):
                rows_scr[c0:c0 + tblk, :] = cmp_buf[slot, kv, :, c0:c0 + tblk].T
        return rows_scr[pl.ds(tok, nch, stride=CMP_STRIDE), :]

    kvcb = _compress(load, nch, w1_ref, pe_ref, w2_ref)

    tq = SUB
    n_slc = -(-(past + t_new) // SLC_BLOCK)
    q = q_ref[...]
    gn = gn_ref[...]
    npos = past + lax.broadcasted_iota(I32, (1, 128), 1)
    ppos = lax.broadcasted_iota(I32, (1, past), 1)
    wpos = past - wb + lax.broadcasted_iota(I32, (1, wb), 1)
    for g in range(NSA_KV):
        rows = slice(g * DH, (g + 1) * DH)
        q4 = _stack_heads(q, g)
        qpos, a_col, s_col = _head_cols(tq, past, g)
        o_cmp, sel = _cmp_branch(q4, kvcb[0][g], kvcb[1][g], ov_ref, qpos, a_col, s_col, tq, past, n_slc)
        sel4 = jnp.concatenate([sel[:, :128]] * NSA_GROUP, axis=0).astype(BF16)

        picked = _dot(sel4, e_ref[...]) > 0.5
        s_p = _bias_mask(_dot(q4, sel_buf[slot, 0, rows, :].astype(BF16)), ppos, picked & (ppos <= qpos), a_col, s_col)
        s_n = _masked_scores(q4, new_scr[:, rows].astype(BF16), npos, npos <= qpos, a_col, s_col)
        o_slc = _softmax_two(s_p, s_n, sel_buf[slot, 1, rows, :].astype(BF16),
                             new_scr[:, 128 + g * DH:128 + (g + 1) * DH].astype(BF16))

        s_p = _bias_mask(_dot(q4, win_ref[0, rows, :].astype(BF16)), wpos, (wpos <= qpos) & (wpos > qpos - WINDOW),
                         a_col, s_col)
        s_n = _masked_scores(q4, new_scr[:, 256 + g * DH:256 + (g + 1) * DH].astype(BF16), npos, npos <= qpos,
                             a_col, s_col)
        o_win = _softmax_two(s_p, s_n, win_ref[1, rows, :].astype(BF16),
                             new_scr[:, 384 + g * DH:384 + (g + 1) * DH].astype(BF16))

        o_ref[:, g * 256:(g + 1) * 256] = _gate_combine(gn, (o_cmp, o_slc, o_win), tq, g)


def _nsa_sample(z3, win, cache_cmp, cache_sel, page_table, w1t, pe_t, w2, ov, e_mat, t_new):
    bs = z3.shape[0]
    n_pages = page_table.shape[1]
    past = n_pages * PAGE
    wb = win.shape[-1]
    const = lambda shape: pl.BlockSpec(shape, lambda bi, pt, n=len(shape): (0,) * n)
    grid_spec = pltpu.PrefetchScalarGridSpec(
        num_scalar_prefetch=1,
        grid=(bs,),
        in_specs=[pl.BlockSpec((None, SUB, 512), lambda bi, pt: (bi, 0, C_Q // 512)),
                  pl.BlockSpec((None, SUB, 128), lambda bi, pt: (bi, 0, C_GN // 128)),
                  pl.BlockSpec((None, SUB, 256), lambda bi, pt: (bi, 0, C_KS // 256)),
                  pl.BlockSpec((None, SUB, 256), lambda bi, pt: (bi, 0, C_KW // 256)),
                  pl.BlockSpec((None, 2, 128, wb), lambda bi, pt: (bi, 0, 0, 0)),
                  pl.BlockSpec(memory_space=pl.ANY),
                  pl.BlockSpec(memory_space=pl.ANY),
                  const(w1t.shape), const(pe_t.shape), const(w2.shape), const(ov.shape), const(e_mat.shape)],
        out_specs=pl.BlockSpec((None, SUB, 512), lambda bi, pt: (bi, 0, 0)),
        scratch_shapes=[pltpu.VMEM((2, 2, 128, past), F32),
                        pltpu.VMEM((2, 2, 128, past), F32),
                        pltpu.VMEM((past, 128), F32),
                        pltpu.VMEM((128, 512), F32),
                        pltpu.SemaphoreType.DMA((2, 2))],
    )
    return pl.pallas_call(
        functools.partial(_nsa_sample_kernel, n_pages=n_pages, past=past, t_new=t_new),
        grid_spec=grid_spec,
        out_shape=jax.ShapeDtypeStruct((bs, SUB, 512), F32),
        compiler_params=_cp(("arbitrary",), 60 * 1024 * 1024),
        name="nsa_sample",
    )(page_table, z3, z3, z3, z3, win, cache_cmp, cache_sel, w1t, pe_t, w2, ov, e_mat)


def _hgrn_kernel(hq_ref, hf_ref, hi_ref, hg_ref, lb_ref, nw_ref, s0_ref, o_ref, sout_ref, st_scr,
                 *, tb, chunk, t_real, blk):
    j = pl.program_id(1)
    nw = nw_ref[...]
    row = lax.broadcasted_iota(I32, (chunk, 1), 0)
    real = row < t_real
    tri = (lax.broadcasted_iota(I32, (chunk, chunk), 0) >= lax.broadcasted_iota(I32, (chunk, chunk), 1)).astype(BF16)
    n_rows = min(chunk, t_real)

    @pl.when(j == 0)
    def _():
        for h in range(HG_HEADS):
            st_scr[h] = s0_ref[h].T

    def head_chunk(rows, h):
        cols = slice(h * HG_DK, (h + 1) * HG_DK)
        lb = lb_ref[h]
        q = _silu(hq_ref[rows, cols])
        f = lb + (1.0 - lb) * _sigmoid(hf_ref[rows, cols])
        k = jnp.where(real, 1.0 - f, 0.0)
        gl = jnp.where(real, jnp.log(f), 0.0)
        v = hi_ref[rows, cols]
        if chunk > 8:
            ga_, gb_, gc_ = _split3(gl)
            bcum = _dot(tri, ga_) + _dot(tri, gb_) + _dot(tri, gc_)
        else:
            bcum = jnp.zeros_like(gl)
            for s in range(t_real):
                bcum = bcum + jnp.where(row >= s, gl[s:s + 1, :], 0.0)
        st = st_scr[h]
        o_inter = _dot_nt((q * jnp.exp(bcum)).astype(BF16), st.astype(BF16))
        v16 = v.astype(BF16)
        parts = []
        for lo in range(0, chunk, blk):
            hi = min(lo + blk, chunk)
            acc = o_inter[lo:hi, :]
            if lo < n_rows:
                qj = q[lo:hi, :]
                bj = bcum[lo:hi, :]
                tj = row[lo:hi, :]
                for s in range(lo, min(hi, t_real)):
                    d = jnp.where(tj >= s, bj - bcum[s:s + 1, :], NEG)
                    w = jnp.sum(qj * jnp.exp(d) * k[s:s + 1, :], axis=-1, keepdims=True)
                    acc = acc + w * v[s:s + 1, :]
                if lo > 0:
                    ref = bcum[lo - 1:lo, :]
                    qt = (qj * jnp.exp(bj - ref)).astype(BF16)
                    kt = (k[0:lo, :] * jnp.exp(ref - bcum[0:lo, :])).astype(BF16)
                    acc = acc + _dot(_dot_nt(qt, kt).astype(BF16), v16[0:lo, :])
            parts.append(acc)
        o = parts[0] if len(parts) == 1 else jnp.concatenate(parts, axis=0)
        bc = bcum[chunk - 1:chunk, :]
        kt = k * jnp.exp(bc - bcum)
        st_scr[h] = st * jnp.exp(bc) + _dot_tn(v16, kt.astype(BF16))
        o = o * lax.rsqrt(jnp.mean(o * o, axis=-1, keepdims=True) + EPS) * nw
        o_ref[rows, cols] = o * _silu(hg_ref[rows, cols])

    def step(ci, carry):
        rows = pl.ds(pl.multiple_of(ci * chunk, chunk), chunk)
        for h in range(HG_HEADS):
            head_chunk(rows, h)
        return carry

    lax.fori_loop(0, tb // chunk, step, 0)

    @pl.when(j == pl.num_programs(1) - 1)
    def _():
        for h in range(HG_HEADS):
            sout_ref[h] = st_scr[h].T


def _hgrn(z3, lb, norm_w, s0, chunk, t_real, tb, blk):
    b, t, _ = z3.shape
    w = HG_HEADS * HG_DK
    col = lambda c0: pl.BlockSpec((None, tb, w), lambda bi, j, c=c0 // w: (bi, j, c))
    state = pl.BlockSpec((None, HG_HEADS, HG_DK, HG_DV), lambda bi, j: (bi, 0, 0, 0))
    return pl.pallas_call(
        functools.partial(_hgrn_kernel, tb=tb, chunk=chunk, t_real=t_real, blk=blk),
        grid=(b, t // tb),
        in_specs=[col(C_HQ), col(C_HF), col(C_HI), col(C_HG),
                  pl.BlockSpec((HG_HEADS, 1, HG_DK), lambda bi, j: (0, 0, 0)),
                  pl.BlockSpec((1, HG_DV), lambda bi, j: (0, 0)),
                  state],
        out_specs=[pl.BlockSpec((None, tb, w), lambda bi, j: (bi, j, 0)), state],
        out_shape=[jax.ShapeDtypeStruct((b, t, w), F32),
                   jax.ShapeDtypeStruct((b, HG_HEADS, HG_DK, HG_DV), F32)],
        scratch_shapes=[pltpu.VMEM((HG_HEADS, HG_DV, HG_DK), F32)],
        compiler_params=_cp(("arbitrary", "arbitrary")),
        name="hgrn2",
    )(z3, z3, z3, z3, lb.reshape(HG_HEADS, 1, HG_DK), norm_w.reshape(1, HG_DV), s0)


def _merge_kernel(x_ref, on_ref, oh_ref, ga_ref, gb_ref, wa_ref, wb_ref, wo_ref, g1_ref, sc_ref, sh_ref,
                  npost_ref, npre_ref, wr_ref, br_ref, y_ref, h_ref, ti_ref, tw_ref):
    a = _dot(on_ref[...].astype(BF16), wa_ref[...])
    bb = _dot(oh_ref[...].astype(BF16), wb_ref[...])
    m = _sigmoid(ga_ref[...]) * a + _sigmoid(gb_ref[...]) * bb
    out = _dot(m.astype(BF16), wo_ref[...])
    y = x_ref[...] + g1_ref[...] * _rms(out, npost_ref[...])
    y_ref[...] = y
    h = _rms(y, npre_ref[...]) * (1.0 + sc_ref[...]) + sh_ref[...]
    h_ref[...] = h
    hh, hl, _ = _split3(h)
    w = wr_ref[...]
    wh = w.astype(BF16)
    wl = (w - wh.astype(F32)).astype(BF16)
    logits = _dot(hh, wh) + _dot(hh, wl) + _dot(hl, wh) + br_ref[...]
    lane = lax.broadcasted_iota(I32, logits.shape, 1)
    lanef = lane.astype(F32)
    logits = jnp.where(lane < N_EXPERTS, logits, -jnp.inf)
    ti = jnp.zeros(logits.shape, F32)
    tv = jnp.zeros(logits.shape, F32)
    v0 = None
    for kk in range(TOP_K):
        mx = jnp.max(logits, axis=-1, keepdims=True)
        idx = jnp.min(jnp.where(logits == mx, lanef, 1e9), axis=-1, keepdims=True)
        v0 = mx if v0 is None else v0
        ti = jnp.where(lane == kk, idx, ti)
        tv = jnp.where(lane == kk, jnp.exp(mx - v0), tv)
        logits = jnp.where(lanef == idx, -jnp.inf, logits)
    ti_ref[...] = ti.astype(I32)
    tw_ref[...] = tv / jnp.sum(tv, axis=-1, keepdims=True)


def _merge(x2d, o_nsa, o_hg, z2d, g1, sc2, sh2, wa, wb, wo, n_post, n_pre, w_router, b_router, rows_per_mod, tm):
    n, d = x2d.shape
    row = lambda w: pl.BlockSpec((tm, w), lambda i: (i, 0))
    if rows_per_mod:
        per = rows_per_mod // tm
        mod_spec = pl.BlockSpec((None, 1, d), lambda i: (i // per, 0, 0))
    else:
        mod_spec = row(d)
    const2 = lambda a: pl.BlockSpec(a.shape, lambda i: (0, 0))
    wr = jnp.pad(w_router, ((0, 0), (0, 128 - N_EXPERTS)))
    br = jnp.pad(b_router, (0, 128 - N_EXPERTS)).reshape(1, 128)
    vec = lambda v: v.reshape(1, d)
    return pl.pallas_call(
        _merge_kernel,
        grid=(n // tm,),
        in_specs=[row(d), row(512), row(512),
                  pl.BlockSpec((tm, d), lambda i: (i, C_GA // 1024)),
                  pl.BlockSpec((tm, d), lambda i: (i, C_GB // 1024)),
                  const2(wa), const2(wb), const2(wo),
                  mod_spec, mod_spec, mod_spec,
                  pl.BlockSpec((1, d), lambda i: (0, 0)), pl.BlockSpec((1, d), lambda i: (0, 0)),
                  const2(wr), const2(br)],
        out_specs=[row(d), row(d), row(128), row(128)],
        out_shape=[jax.ShapeDtypeStruct((n, d), F32), jax.ShapeDtypeStruct((n, d), F32),
                   jax.ShapeDtypeStruct((n, 128), I32), jax.ShapeDtypeStruct((n, 128), F32)],
        compiler_params=_cp(("arbitrary",)),
        name="merge_router",
    )(x2d, o_nsa, o_hg, z2d, z2d, wa, wb, wo, g1, sc2, sh2, vec(n_post), vec(n_pre), wr, br)


def _moe_kernel(wt_ref, we_ref, lo_ref, hi_ref, ni_ref, tok_ref, tok_next_ref, rw_ref, h_hbm,
                wu_ref, bu_ref, wd_ref, bd_ref, o_ref, xbuf, sem, *, tm, n_tiles):
    w = pl.program_id(0)
    tile = wt_ref[w]
    first = (w == 0) | (tile != wt_ref[jnp.maximum(w - 1, 0)])
    valid = w < ni_ref[0]
    slot = tile % 2

    has_next = tile + 1 < n_tiles

    def row_copy(idx_ref, sl, r):
        return pltpu.make_async_copy(h_hbm.at[pl.ds(idx_ref[r], 1), :], xbuf.at[sl, pl.ds(r, 1), :], sem.at[sl])

    @pl.when(valid & (w == 0))
    def _():
        def body(r, c):
            row_copy(tok_ref, 0, r).start()
            return c
        lax.fori_loop(0, tm, body, 0, unroll=8)

    def ffn(issue_next, init):
        if init:
            pltpu.make_async_copy(h_hbm.at[pl.ds(0, tm), :], xbuf.at[slot], sem.at[slot]).wait()
        x = xbuf[slot].astype(BF16)
        if issue_next:
            for r in range(tm):
                row_copy(tok_next_ref, 1 - slot, r).start()
        z = _dot(x, wu_ref[...]) + bu_ref[...]
        de = wd_ref.shape[0]
        gate = jnp.minimum(z[:, :de], SWIGLU_LIMIT)
        up = jnp.clip(z[:, de:], -SWIGLU_LIMIT, SWIGLU_LIMIT)
        act = (up + 1.0) * gate * _sigmoid(SWIGLU_ALPHA * gate)
        y = _dot(act.astype(BF16), wd_ref[...]) + bd_ref[...]
        row = lax.broadcasted_iota(I32, (tm, 1), 0)
        mine = (row >= lo_ref[w]) & (row < hi_ref[w])
        contrib = jnp.where(mine, y * rw_ref[...], 0.0)
        o_ref[...] = contrib if init else o_ref[...] + contrib

    @pl.when(valid & first & has_next)
    def _():
        ffn(True, True)

    @pl.when(valid & first & jnp.logical_not(has_next))
    def _():
        ffn(False, True)

    @pl.when(valid & jnp.logical_not(first))
    def _():
        ffn(False, False)


def _moe_ffn(h2, items, row_token, row_weight, w_up, b_up, w_down, b_down, tm):
    n, d = h2.shape
    n_rows = row_token.shape[0]
    n_tiles = n_rows // tm
    ne, _, dh2 = w_up.shape
    wt, we, lo, hi, ni = items
    grid_spec = pltpu.PrefetchScalarGridSpec(
        num_scalar_prefetch=5,
        grid=(wt.shape[0],),
        in_specs=[pl.BlockSpec((tm,), lambda w, wt, *_: (wt[w],), memory_space=pltpu.SMEM),
                  pl.BlockSpec((tm,), lambda w, wt, *_: (jnp.minimum(wt[w] + 1, n_tiles - 1),), memory_space=pltpu.SMEM),
                  pl.BlockSpec((tm, 1), lambda w, wt, *_: (wt[w], 0)),
                  pl.BlockSpec(memory_space=pl.ANY),
                  pl.BlockSpec((None, d, dh2), lambda w, wt, we, *_: (we[w], 0, 0)),
                  pl.BlockSpec((None, 1, dh2), lambda w, wt, we, *_: (we[w], 0, 0)),
                  pl.BlockSpec((None, dh2 // 2, d), lambda w, wt, we, *_: (we[w], 0, 0)),
                  pl.BlockSpec((None, 1, d), lambda w, wt, we, *_: (we[w], 0, 0))],
        out_specs=pl.BlockSpec((tm, d), lambda w, wt, *_: (wt[w], 0)),
        scratch_shapes=[pltpu.VMEM((2, tm, d), F32), pltpu.SemaphoreType.DMA((2,))],
    )
    return pl.pallas_call(
        functools.partial(_moe_kernel, tm=tm, n_tiles=n_tiles),
        grid_spec=grid_spec,
        out_shape=jax.ShapeDtypeStruct((n_rows, d), F32),
        compiler_params=_cp(("arbitrary",)),
        name="moe_ffn",
    )(wt, we, lo, hi, ni, row_token, row_token, row_weight, h2, w_up, b_up.reshape(ne, 1, dh2),
      w_down, b_down.reshape(ne, 1, d))


def _combine_kernel(pos_ref, y_hbm, y1_ref, g2_ref, npost_ref, o_ref, buf, sem, *, tm):
    def body(r, c):
        for kk in range(TOP_K):
            pltpu.make_async_copy(y_hbm.at[pl.ds(pos_ref[r * TOP_K + kk], 1), :], buf.at[kk, pl.ds(r, 1), :], sem.at[0]).start()
        return c
    lax.fori_loop(0, tm, body, 0, unroll=4)
    for kk in range(TOP_K):
        pltpu.make_async_copy(y_hbm.at[pl.ds(0, tm), :], buf.at[kk], sem.at[0]).wait()
    moe = (buf[0] + buf[1]) + (buf[2] + buf[3])
    o_ref[...] = y1_ref[...] + g2_ref[...] * _rms(moe, npost_ref[...])


def _combine(pos_flat, y_sorted, y1, g2, n_post, rows_per_mod, tm):
    n, d = y1.shape
    if rows_per_mod:
        per = rows_per_mod // tm
        mod_spec = pl.BlockSpec((None, 1, d), lambda i: (i // per, 0, 0))
    else:
        mod_spec = pl.BlockSpec((tm, d), lambda i: (i, 0))
    return pl.pallas_call(
        functools.partial(_combine_kernel, tm=tm),
        grid=(n // tm,),
        in_specs=[pl.BlockSpec((tm * TOP_K,), lambda i: (i,), memory_space=pltpu.SMEM),
                  pl.BlockSpec(memory_space=pl.ANY),
                  pl.BlockSpec((tm, d), lambda i: (i, 0)),
                  mod_spec,
                  pl.BlockSpec((1, d), lambda i: (0, 0))],
        out_specs=pl.BlockSpec((tm, d), lambda i: (i, 0)),
        out_shape=jax.ShapeDtypeStruct((n, d), F32),
        scratch_shapes=[pltpu.VMEM((TOP_K, tm, d), F32), pltpu.SemaphoreType.DMA((1,))],
        compiler_params=_cp(("arbitrary",)),
        name="moe_combine",
    )(pos_flat, y_sorted, y1, g2, n_post.reshape(1, d))


def _route_tables(top_i, top_w, tm):
    m = top_i.shape[0] * TOP_K
    m_pad = -(-m // tm) * tm
    n_tiles = m_pad // tm
    e_flat = top_i.reshape(-1).astype(I32)
    iota = jnp.arange(m, dtype=I32)
    _, order, w_sorted = lax.sort((e_flat, iota, top_w.reshape(-1)), num_keys=1, is_stable=True)
    _, pos_of = lax.sort((order, iota), num_keys=1)
    row_token = jnp.pad(order // TOP_K, (0, m_pad - m))
    row_weight = jnp.pad(w_sorted, (0, m_pad - m)).reshape(m_pad, 1)
    ex = jnp.arange(N_EXPERTS, dtype=I32)
    counts = jnp.sum((e_flat[:, None] == ex[None, :]).astype(I32), axis=0)
    uend = jnp.cumsum(counts)
    ustart = uend - counts
    first_tile = ustart // tm
    n_item_e = jnp.where(counts > 0, (uend - 1) // tm - first_tile + 1, 0)
    iend = jnp.cumsum(n_item_e)
    n_items = iend[-1]
    wid = jnp.arange(n_tiles + N_EXPERTS - 1, dtype=I32)
    we = jnp.minimum(jnp.sum((wid[:, None] >= iend[None, :]).astype(I32), axis=1), N_EXPERTS - 1)
    onehot = (we[:, None] == ex[None, :]).astype(I32)
    pick = lambda v: jnp.sum(onehot * v[None, :], axis=1)
    wt = pick(first_tile) + (wid - pick(iend - n_item_e))
    lo = jnp.maximum(pick(ustart), wt * tm) - wt * tm
    hi = jnp.minimum(pick(uend), (wt + 1) * tm) - wt * tm
    live = wid < n_items
    last_e = jnp.max(jnp.where(counts > 0, ex, 0))
    items = (jnp.where(live, wt, n_tiles - 1), jnp.where(live, we, last_e),
             jnp.where(live, lo, 0), jnp.where(live, hi, 0), n_items.reshape(1))
    return tuple(a.astype(I32) for a in items), row_token, row_weight, pos_of


def _overlap_table(n_rows, n_cmp, n_cols):
    start = np.arange(n_rows)[:, None] * CMP_STRIDE
    j0 = np.arange(n_cols)[None, :] * SLC_BLOCK
    ov = (start < j0 + SLC_BLOCK) & (start + CMP_BLOCK > j0) & (np.arange(n_rows)[:, None] < n_cmp)
    return jnp.asarray(ov.astype(np.float32), dtype=BF16)


def _expand_table(n_rows, n_keys):
    e = (np.arange(n_keys)[None, :] // SLC_BLOCK) == np.arange(n_rows)[:, None]
    return jnp.asarray(e.astype(np.float32), dtype=BF16)


def _pack_w_in(w_in):
    d = w_in.shape[0]
    q = w_in[:, 0:512]
    kv = w_in[:, 512:1280]
    gn = w_in[:, 1280:1304]
    hh = w_in[:, 1304:3352]
    ga = w_in[:, 3352:4376]
    gb = w_in[:, 4376:5400]
    z = lambda w: jnp.zeros((d, w), w_in.dtype)
    return jnp.concatenate([ga, gb, q, hh, kv, gn, z(ZW - C_GN - 24)], axis=1).astype(BF16)


def _pack_compress(cmp_pe, cmp_w1, cmp_w2):
    r = CMP_BLOCK // CMP_STRIDE
    wt = cmp_w1.reshape(2, r, CMP_STRIDE, DH, CMP_HIDDEN).transpose(0, 2, 3, 1, 4).reshape(2, CMP_STRIDE, DH, r * CMP_HIDDEN)
    zero = jnp.zeros_like(wt)
    w1bd = jnp.concatenate([jnp.concatenate([wt, zero], axis=-1), jnp.concatenate([zero, wt], axis=-1)], axis=2)
    w1bd = w1bd.reshape(2, CMP_STRIDE * 2 * DH, 2 * r * CMP_HIDDEN)
    pe_t = jnp.pad(cmp_pe.reshape(2, r, CMP_STRIDE, DH), ((0, 0), (0, SUB - r), (0, 0), (0, DH)))
    return w1bd.astype(BF16), pe_t.reshape(2, SUB, CMP_STRIDE * 2 * DH), cmp_w2


def _kv_out(z3, c0, rows):
    b = z3.shape[0]
    return z3[:, rows, c0:c0 + 256].reshape(b, -1, 2, NSA_KV, DH)[None]


def kernel(x_prompt, x_sample, c_prompt, c_sample, cache_cmp, cache_sel, state_win, state_hgrn, page_table, w_ada, b_ada, norm_mix_pre, norm_mix_post, norm_ffn_pre, norm_ffn_post, w_in, cmp_pe, cmp_w1, cmp_w2, hg_lb_logits, hg_norm, w_branch_a, w_branch_b, w_out, w_router, b_router, w_up, b_up, w_down, b_down):
    bp, t, d = x_prompt.shape
    bs, ts, _ = x_sample.shape
    depth = w_in.shape[0]
    assert depth == 1 and ts <= SUB
    n_pool = cache_cmp.shape[1]
    past = page_table.shape[1] * PAGE
    lb_all = jnp.cumsum(jax.nn.softmax(hg_lb_logits.astype(F32), axis=0), axis=0)

    l = 0
    w_packed = _pack_w_in(w_in[l])
    w1t, pe_t, w2c = _pack_compress(cmp_pe[l], cmp_w1[l], cmp_w2[l])
    wa, wb, wo = w_branch_a[l].astype(BF16), w_branch_b[l].astype(BF16), w_out[l].astype(BF16)
    wu, wd = w_up[l].astype(BF16), w_down[l].astype(BF16)

    mod = _modulation(jnp.concatenate([c_prompt, c_sample], axis=0), w_ada[l], b_ada[l])
    mods = [m[:, None, :] for m in jnp.split(mod, 6, axis=-1)]
    sh1, sc1, g1, sh2, sc2, g2 = mods
    p_, s_ = slice(0, bp), slice(bp, bp + bs)

    tm_p = 256 if t % 256 == 0 else t
    zp = _inproj(x_prompt.reshape(bp * t, d), norm_mix_pre[l], sc1[p_], sh1[p_], w_packed, t, tm_p)
    zp3 = zp.reshape(bp, t, ZW)
    nch = t // CMP_STRIDE
    n_slc = -(-t // SLC_BLOCK)
    kvcb_p = _compress_prompt(zp3, w1t, pe_t, w2c)
    tq = 256 if t % 256 == 0 else t
    tk = 512 if t % 512 == 0 else t
    o_nsa_p = _nsa_prompt(zp3, kvcb_p, _overlap_table(nch, nch - 1, 128), _expand_table(128, t), tq, tk)
    s0_p = jnp.zeros((bp, HG_HEADS, HG_DK, HG_DV), F32)
    chunk = int(np.gcd(t, HG_CHUNK))
    tb_h = 512 if t % 512 == 0 else t
    o_hg_p, hg_state_p = _hgrn(zp3, lb_all[l], hg_norm[l], s0_p, chunk, chunk, tb_h, 16 if chunk % 16 == 0 else chunk)
    tm_m = 512 if t % 512 == 0 else t
    y1_p, h2_p, ti_p, tw_p = _merge(x_prompt.reshape(bp * t, d), o_nsa_p.reshape(bp * t, 512), o_hg_p.reshape(bp * t, 512),
                                    zp, g1[p_], sc2[p_], sh2[p_], wa, wb, wo, norm_mix_post[l], norm_ffn_pre[l],
                                    w_router[l], b_router[l], t, tm_m)

    xs = jnp.pad(x_sample, ((0, 0), (0, SUB - ts), (0, 0))).reshape(bs * SUB, d)
    rep = lambda m: jnp.broadcast_to(m[s_], (bs, SUB, d)).reshape(bs * SUB, d)
    tm_s = 512 if (bs * SUB) % 512 == 0 else bs * SUB
    zs = _inproj(xs, norm_mix_pre[l], rep(sc1), rep(sh1), w_packed, 0, min(tm_s, 256))
    zs3 = zs.reshape(bs, SUB, ZW)
    tok_minor = lambda a: jnp.transpose(a, (0, 2, 3, 4, 1)).reshape(a.shape[0], 2, NSA_KV * DH, a.shape[1])
    o_nsa_s = _nsa_sample(zs3, tok_minor(state_win[l]), tok_minor(cache_cmp[l]), tok_minor(cache_sel[l]),
                          page_table, w1t, pe_t, w2c,
                          _overlap_table(past // CMP_STRIDE, past // CMP_STRIDE - 1, 256),
                          _expand_table(128, past), ts)
    o_hg_s, hg_state_s = _hgrn(zs3, lb_all[l], hg_norm[l], state_hgrn[l], SUB, ts, SUB, SUB)
    y1_s, h2_s, ti_s, tw_s = _merge(xs, o_nsa_s.reshape(bs * SUB, 512), o_hg_s.reshape(bs * SUB, 512), zs,
                                    rep(g1), rep(sc2), rep(sh2), wa, wb, wo, norm_mix_post[l], norm_ffn_pre[l],
                                    w_router[l], b_router[l], 0, tm_s)
    real = lambda a: a.reshape(bs, SUB, -1)[:, :ts].reshape(bs * ts, -1)

    h2 = jnp.concatenate([h2_p, real(h2_s)], axis=0)
    top_i = jnp.concatenate([ti_p[:, :TOP_K], real(ti_s)[:, :TOP_K]], axis=0)
    top_w = jnp.concatenate([tw_p[:, :TOP_K], real(tw_s)[:, :TOP_K]], axis=0)
    tm_e = 512
    items, row_token, row_weight, pos_of = _route_tables(top_i, top_w, tm_e)
    y_sorted = _moe_ffn(h2, items, row_token, row_weight, wu, b_up[l], wd, b_down[l], tm_e)
    np_tok = bp * t
    tm_cp = 256 if t % 256 == 0 else t
    y_p = _combine(pos_of[:np_tok * TOP_K], y_sorted, y1_p, g2[p_], norm_ffn_post[l], t, tm_cp).reshape(bp, t, d)
    g2_s = jnp.broadcast_to(g2[s_], (bs, ts, d)).reshape(bs * ts, d)
    tm_cs = 256 if (bs * ts) % 256 == 0 else bs * ts
    y_s = _combine(pos_of[np_tok * TOP_K:], y_sorted, real(y1_s), g2_s, norm_ffn_post[l], 0, tm_cs).reshape(bs, ts, d)
    keep = min(WINDOW, t)
    new_rows = zs3[:, :ts, C_KW:C_KW + 256].reshape(bs, ts, 2, NSA_KV, DH)
    win_s = jnp.concatenate([state_win[l], new_rows], axis=1)[:, ts:][None]
    return (y_p, y_s,
            _kv_out(zp3, C_KC, slice(0, t)), _kv_out(zs3, C_KC, slice(0, ts)),
            _kv_out(zp3, C_KS, slice(0, t)), _kv_out(zs3, C_KS, slice(0, ts)),
            _kv_out(zp3, C_KW, slice(t - keep, t)), win_s,
            hg_state_p[None], hg_state_s[None])
```

```python
import functools

import numpy as np
import jax
import jax.numpy as jnp
from jax import lax
from jax.experimental import pallas as pl
from jax.experimental.pallas import tpu as pltpu

F32, BF16, I32 = jnp.float32, jnp.bfloat16, jnp.int32

NSA_HEADS, NSA_KV, NSA_GROUP, DH = 8, 2, 4, 64
CMP_BLOCK, CMP_STRIDE, CMP_HIDDEN = 32, 16, 128
SLC_BLOCK, SLC_TOPK, WINDOW = 64, 16, 512
HG_HEADS, HG_DK, HG_DV, HG_CHUNK = 4, 128, 128, 64
N_EXPERTS, TOP_K = 32, 4
SWIGLU_LIMIT, SWIGLU_ALPHA = 7.0, 1.702
EPS, NEG, FORCE = 1e-6, -1e30, 1e4
PAGE = 128

C_GA, C_GB, C_Q, C_HQ, C_HF, C_HI, C_HG = 0, 1024, 2048, 2560, 3072, 3584, 4096
C_KC, C_KS, C_KW, C_GN, ZW = 4608, 4864, 5120, 5376, 5632
SUB = 8

VMEM_LIMIT = 56 * 1024 * 1024


def _cp(sem, vmem=VMEM_LIMIT):
    return pltpu.CompilerParams(dimension_semantics=sem, vmem_limit_bytes=vmem)


def _dot(a, b):
    return jnp.dot(a, b, preferred_element_type=F32)


def _dot_nt(a, b):
    return lax.dot_general(a, b, (((1,), (1,)), ((), ())), preferred_element_type=F32)


def _dot_tn(a, b):
    return lax.dot_general(a, b, (((0,), (0,)), ((), ())), preferred_element_type=F32)


def _split3(x):
    a = x.astype(BF16)
    r = x - a.astype(F32)
    b = r.astype(BF16)
    c = (r - b.astype(F32)).astype(BF16)
    return a, b, c


def _dot3(x, w_bf16):
    a, b, c = _split3(x)
    return _dot(a, w_bf16) + _dot(b, w_bf16) + _dot(c, w_bf16)


def _sigmoid(x):
    return 1.0 / (1.0 + jnp.exp(-x))


def _silu(x):
    return x * _sigmoid(x)


def _rms(x, g):
    return x * lax.rsqrt(jnp.mean(x * x, axis=-1, keepdims=True) + EPS) * g


def _gelu_tanh(x):
    return 0.5 * x * (1.0 + jnp.tanh(0.7978845608028654 * (x + 0.044715 * (x * x * x))))


def _mod_kernel(c_ref, w_ref, b_ref, o_ref):
    o_ref[...] = _dot(_silu(c_ref[...]).astype(BF16), w_ref[...].astype(BF16)) + b_ref[...]


def _modulation(c, w_ada, b_ada):
    m, d = c.shape
    nw = w_ada.shape[1]
    tn = nw // 6
    return pl.pallas_call(
        _mod_kernel,
        grid=(nw // tn,),
        in_specs=[pl.BlockSpec((m, d), lambda j: (0, 0)),
                  pl.BlockSpec((d, tn), lambda j: (0, j)),
                  pl.BlockSpec((1, tn), lambda j: (0, j))],
        out_specs=pl.BlockSpec((m, tn), lambda j: (0, j)),
        out_shape=jax.ShapeDtypeStruct((m, nw), F32),
        compiler_params=_cp(("arbitrary",)),
        name="modulation",
    )(c, w_ada, b_ada.reshape(1, nw))


INPROJ_TN = 512


def _inproj_kernel(x_ref, g_ref, sc_ref, sh_ref, w_ref, o_ref):
    h = (_rms(x_ref[...], g_ref[...]) * (1.0 + sc_ref[...]) + sh_ref[...]).astype(BF16)
    for c0 in range(0, o_ref.shape[1], INPROJ_TN):
        o_ref[:, c0:c0 + INPROJ_TN] = _dot(h, w_ref[:, c0:c0 + INPROJ_TN])


def _inproj(x2d, g, sc, sh, w_packed, rows_per_mod, tm):
    n, d = x2d.shape
    zw = w_packed.shape[1]
    if rows_per_mod:
        per = rows_per_mod // tm
        mod_spec = pl.BlockSpec((None, 1, d), lambda i: (i // per, 0, 0))
    else:
        mod_spec = pl.BlockSpec((tm, d), lambda i: (i, 0))
    return pl.pallas_call(
        _inproj_kernel,
        grid=(n // tm,),
        in_specs=[pl.BlockSpec((tm, d), lambda i: (i, 0)),
                  pl.BlockSpec((1, d), lambda i: (0, 0)),
                  mod_spec, mod_spec,
                  pl.BlockSpec((d, zw), lambda i: (0, 0))],
        out_specs=pl.BlockSpec((tm, zw), lambda i: (i, 0)),
        out_shape=jax.ShapeDtypeStruct((n, zw), F32),
        compiler_params=_cp(("arbitrary",)),
        name="inproj",
    )(x2d, g.reshape(1, d), sc, sh, w_packed)


def _compress(load, nch, w1_ref, pe_ref, w2_ref):
    out = []
    for kv in range(2):
        w1 = w1_ref[kv]
        x = jnp.concatenate([load(kv, tok) for tok in range(CMP_STRIDE)], axis=1).astype(BF16)
        acc = _dot(x, w1)
        pe = _dot(pe_ref[kv].astype(BF16), w1)
        pre0 = pe[0:1, :CMP_HIDDEN] + pe[1:2, CMP_HIDDEN:2 * CMP_HIDDEN]
        w2 = w2_ref[kv].astype(BF16)
        row = []
        for g in range(2):
            c0 = g * 2 * CMP_HIDDEN
            pre = (acc[:, c0:c0 + CMP_HIDDEN]
                   + pltpu.roll(acc[:, c0 + CMP_HIDDEN:c0 + 2 * CMP_HIDDEN], nch - 1, 0) + pre0)
            row.append(_dot(_gelu_tanh(pre).astype(BF16), w2))
        out.append(row)
    return out


def _compress_prompt_kernel(k_ref, v_ref, w1_ref, pe_ref, w2_ref, o_ref, *, nch):
    def load(kv, tok):
        return (k_ref, v_ref)[kv][pl.ds(tok, nch, stride=CMP_STRIDE), :]

    out = _compress(load, nch, w1_ref, pe_ref, w2_ref)
    for kv in range(2):
        for g in range(2):
            o_ref[kv, g] = out[kv][g]


def _compress_prompt(z3, w1t, pe_t, w2):
    b, t, _ = z3.shape
    nch = t // CMP_STRIDE
    return pl.pallas_call(
        functools.partial(_compress_prompt_kernel, nch=nch),
        grid=(b,),
        in_specs=[pl.BlockSpec((None, t, 128), lambda i: (i, 0, C_KC // 128)),
                  pl.BlockSpec((None, t, 128), lambda i: (i, 0, C_KC // 128 + 1)),
                  pl.BlockSpec(w1t.shape, lambda i: (0, 0, 0)),
                  pl.BlockSpec(pe_t.shape, lambda i: (0, 0, 0)),
                  pl.BlockSpec(w2.shape, lambda i: (0, 0, 0))],
        out_specs=pl.BlockSpec((None, 2, 2, nch, DH), lambda i: (i, 0, 0, 0, 0)),
        out_shape=jax.ShapeDtypeStruct((b, 2, 2, nch, DH), F32),
        compiler_params=_cp(("arbitrary",)),
        name="compress_prompt",
    )(z3, z3, w1t, pe_t, w2)


def _slope(g, r):
    return 2.0 ** (-(g * NSA_GROUP + r + 1))


def _stack_heads(q, g):
    parts = [q[:, (g * NSA_GROUP + r) * DH:(g * NSA_GROUP + r + 1) * DH] for r in range(NSA_GROUP)]
    return (jnp.concatenate(parts, axis=0) * (DH ** -0.5)).astype(BF16)


def _head_cols(tq, q0, g):
    pos1 = q0 + lax.broadcasted_iota(I32, (tq, 1), 0)
    qpos = jnp.concatenate([pos1] * NSA_GROUP, axis=0)
    slope = jnp.concatenate([jnp.full((tq, 1), _slope(g, r), F32) for r in range(NSA_GROUP)], axis=0)
    return qpos, -slope * qpos.astype(F32), slope


def _bias_mask(s, kpos_row, valid, a_col, s_col):
    return jnp.where(valid, s + (a_col + s_col * kpos_row.astype(F32)), NEG)


def _masked_scores(q4, k_bf16, kpos_row, valid, a_col, s_col):
    return _bias_mask(_dot_nt(q4, k_bf16), kpos_row, valid, a_col, s_col)


def _softmax_two(s_p, s_n, vt_p, v_n):
    m = jnp.maximum(jnp.max(s_p, axis=-1, keepdims=True), jnp.max(s_n, axis=-1, keepdims=True))
    p_p = jnp.exp(s_p - m)
    p_n = jnp.exp(s_n - m)
    l = jnp.sum(p_p, axis=-1, keepdims=True) + jnp.sum(p_n, axis=-1, keepdims=True)
    return (_dot_nt(p_p.astype(BF16), vt_p) + _dot(p_n.astype(BF16), v_n)) / l


def _topk_mask(score, k, n):
    lane = lax.broadcasted_iota(I32, score.shape, 1)
    ahead = jnp.zeros(score.shape, F32)
    for i in range(n):
        col = score[:, i:i + 1]
        ahead = ahead + jnp.where((col > score) | ((col == score) & (lane > i)), 1.0, 0.0)
    return jnp.where((ahead < k) & (lane < n), 1.0, 0.0)


def _cmp_attend(q4, kcb, vcb, qpos, a_col, s_col, tq):
    ncmp = kcb.shape[0]
    endpos = lax.broadcasted_iota(I32, (1, ncmp), 1) * CMP_STRIDE + (CMP_BLOCK - 1)
    valid = endpos <= qpos
    s = _masked_scores(q4, kcb.astype(BF16), endpos, valid, a_col, s_col)
    m = jnp.max(s, axis=-1, keepdims=True)
    e = jnp.exp(s - m)
    p = jnp.where(valid, e, 0.0) / jnp.sum(e, axis=-1, keepdims=True)
    o = _dot(p.astype(BF16), vcb.astype(BF16))
    return o, p[0:tq] + p[tq:2 * tq] + p[2 * tq:3 * tq] + p[3 * tq:4 * tq]


def _cmp_branch(q4, kcb, vcb, ov_ref, qpos, a_col, s_col, tq, q0, n_slc):
    o, psum = _cmp_attend(q4, kcb, vcb, qpos, a_col, s_col, tq)
    imp = _dot3(psum, ov_ref[...])
    jl = lax.broadcasted_iota(I32, imp.shape, 1)
    cur = lax.shift_right_logical(q0 + lax.broadcasted_iota(I32, (tq, 1), 0), 6)
    forced = (jl == 0) | (jl == cur) | (jl == cur - 1)
    score = jnp.where(forced, FORCE, jnp.where(jl <= cur, imp, -FORCE))
    return o, _topk_mask(score, min(SLC_TOPK, n_slc), n_slc)


def _select_cols(imp, tq, q0, n_slc):
    jr = lax.broadcasted_iota(I32, imp.shape, 0)
    cur = lax.shift_right_logical(q0 + lax.broadcasted_iota(I32, (1, tq), 1), 6)
    forced = (jr == 0) | (jr == cur) | (jr == cur - 1)
    score = jnp.where(forced, FORCE, jnp.where(jr <= cur, imp, -FORCE))
    ahead = jnp.zeros(score.shape, F32)
    for i in range(n_slc):
        rw = score[i:i + 1, :]
        ahead = ahead + jnp.where((rw > score) | ((rw == score) & (jr > i)), 1.0, 0.0)
    return jnp.where((ahead < min(SLC_TOPK, n_slc)) & (jr < n_slc), 1.0, 0.0)


def _online(s, carry, v16):
    m, l, acc = carry
    m_new = jnp.maximum(m, jnp.max(s, axis=-1, keepdims=True))
    alpha = jnp.exp(m - m_new)
    p = jnp.exp(s - m_new)
    return m_new, alpha * l + jnp.sum(p, axis=-1, keepdims=True), alpha * acc + _dot(p.astype(BF16), v16)


def _gate_combine(gn, outs, tq, g):
    sig = _sigmoid(gn)
    heads = []
    for r in range(NSA_GROUP):
        acc = None
        for br, o in enumerate(outs):
            c = br * NSA_HEADS + g * NSA_GROUP + r
            t = sig[:, c:c + 1] * o[r * tq:(r + 1) * tq, :]
            acc = t if acc is None else acc + t
        heads.append(acc)
    return jnp.concatenate(heads, axis=1)


MASK_BIG = 2.0 ** 60
N_SEL_COLS = 32
COL_POS_HI, COL_POS_LO = 32, 33


def _nsa_prompt_kernel(q_ref, gn_ref, kvc_ref, ks_ref, vs_ref, kw_ref, vw_ref, ovt_ref, o_ref,
                       ksa_scr, kwa_scr, vst_scr, vwt_scr, *, tq, t_len, n_slc):
    i = pl.program_id(1)
    q0 = i * tq
    nr = NSA_GROUP * tq

    @pl.when(i == 0)
    def _():
        kp = lax.broadcasted_iota(I32, (t_len, DH), 0)
        ln = lax.broadcasted_iota(I32, (t_len, DH), 1)
        pos_cols = jnp.where(ln == COL_POS_HI, lax.shift_right_logical(kp, 7).astype(F32),
                             jnp.where(ln == COL_POS_LO, (kp & 127).astype(F32), 0.0))
        sel_cols = jnp.where(ln == lax.shift_right_logical(kp, 6), 1.0, pos_cols)
        for g in range(NSA_KV):
            lanes = slice(g * DH, (g + 1) * DH)
            ksa_scr[g] = jnp.concatenate([ks_ref[:, lanes], sel_cols], axis=1).astype(BF16)
            kwa_scr[g] = jnp.concatenate([kw_ref[:, lanes], pos_cols], axis=1).astype(BF16)
        for c0 in range(0, t_len, tq):
            vst_scr[:, c0:c0 + tq] = vs_ref[c0:c0 + tq, :].T.astype(BF16)
            vwt_scr[:, c0:c0 + tq] = vw_ref[c0:c0 + tq, :].T.astype(BF16)

    lane = lax.broadcasted_iota(I32, (1, nr), 1)
    qpos = q0 + (lane & (tq - 1))
    head = lax.shift_right_logical(lane, tq.bit_length() - 1)
    krow = lax.broadcasted_iota(I32, (tq, 1), 0)
    sig_t = _sigmoid(gn_ref[...].T)
    init = (jnp.full((1, nr), NEG, F32), jnp.zeros((1, nr), F32), jnp.zeros((DH, nr), F32))
    for g in range(NSA_KV):
        rows = slice(g * DH, (g + 1) * DH)
        slope = jnp.where(head == 0, _slope(g, 0), jnp.where(head == 1, _slope(g, 1),
                          jnp.where(head == 2, _slope(g, 2), _slope(g, 3))))
        qt = q_ref[:, g * 256:(g + 1) * 256].T * (DH ** -0.5)
        q4t = jnp.concatenate([qt[r * DH:(r + 1) * DH, :] for r in range(NSA_GROUP)], axis=1)

        ncmp = kvc_ref.shape[2]
        endpos = lax.broadcasted_iota(I32, (ncmp, 1), 0) * CMP_STRIDE + (CMP_BLOCK - 1)
        valid = endpos <= qpos
        s = _dot(kvc_ref[0, g].astype(BF16), q4t.astype(BF16)) + slope * endpos.astype(F32)
        s = jnp.where(valid, s, NEG)
        e = jnp.exp(s - jnp.max(s, axis=0, keepdims=True))
        p = jnp.where(valid, e, 0.0) / jnp.sum(e, axis=0, keepdims=True)
        o_cmp = _dot_tn(kvc_ref[1, g].astype(BF16), p.astype(BF16))
        psum_t = p[:, 0:tq] + p[:, tq:2 * tq] + p[:, 2 * tq:3 * tq] + p[:, 3 * tq:4 * tq]
        imp_t = sum(_dot(ovt_ref[...], part) for part in _split3(psum_t))
        sel_t = _select_cols(imp_t, tq, q0, n_slc)
        sel4 = jnp.concatenate([sel_t] * NSA_GROUP, axis=1)
        jr = lax.broadcasted_iota(I32, (N_SEL_COLS, nr), 0)
        pos_rows = jnp.where(jr == COL_POS_HI - N_SEL_COLS, 128.0 * slope,
                             jnp.where(jr == COL_POS_LO - N_SEL_COLS, slope, 0.0))
        pieces = [q4t, (sel4 - 1.0) * MASK_BIG]
        if sel4.shape[0] < N_SEL_COLS:
            pieces.append(jnp.zeros((N_SEL_COLS - sel4.shape[0], nr), F32))
        qa = jnp.concatenate(pieces + [pos_rows], axis=0).astype(BF16)

        def attend(ka_ref, vt_ref, kt_lo, band):
            def tile(kt, carry, diag):
                k0 = pl.multiple_of(kt * tq, tq)
                s = _dot(ka_ref[g, pl.ds(k0, tq), :], qa)
                if band:
                    s = jnp.where(k0 + krow > qpos - WINDOW, s, NEG)
                if diag:
                    s = jnp.where(k0 + krow <= qpos, s, NEG)
                m, l, acc = carry
                m_new = jnp.maximum(m, jnp.max(s, axis=0, keepdims=True))
                alpha = jnp.exp(m - m_new)
                pt = jnp.exp(s - m_new)
                l = alpha * l + jnp.sum(pt, axis=0, keepdims=True)
                acc = alpha * acc + _dot(vt_ref[rows, pl.ds(k0, tq)], pt.astype(BF16))
                return m_new, l, acc

            carry = lax.fori_loop(kt_lo, i, lambda kt, c: tile(kt, c, False), init)
            _, l, acc = tile(i, carry, True)
            return acc / l

        o_slc = attend(ksa_scr, vst_scr, 0, False)
        o_win = attend(kwa_scr, vwt_scr, jnp.maximum(i - WINDOW // tq, 0), True)

        heads = []
        for r in range(NSA_GROUP):
            cols = slice(r * tq, (r + 1) * tq)
            acc = None
            for br, o in enumerate((o_cmp, o_slc, o_win)):
                c = br * NSA_HEADS + g * NSA_GROUP + r
                t = sig_t[c:c + 1, :] * o[:, cols]
                acc = t if acc is None else acc + t
            heads.append(acc)
        o_ref[:, g * 256:(g + 1) * 256] = jnp.concatenate(heads, axis=0).T


def _nsa_prompt(z3, kvcb, ovt, tq):
    b, t, _ = z3.shape
    n_slc = -(-t // SLC_BLOCK)
    assert n_slc <= N_SEL_COLS and WINDOW % tq == 0 and t % tq == 0 and tq & (tq - 1) == 0
    kv_spec = lambda col: pl.BlockSpec((None, t, 128), lambda bi, i, c=col // 128: (bi, 0, c))
    return pl.pallas_call(
        functools.partial(_nsa_prompt_kernel, tq=tq, t_len=t, n_slc=n_slc),
        grid=(b, t // tq),
        in_specs=[pl.BlockSpec((None, tq, 512), lambda bi, i: (bi, i, C_Q // 512)),
                  pl.BlockSpec((None, tq, 128), lambda bi, i: (bi, i, C_GN // 128)),
                  pl.BlockSpec((None,) + kvcb.shape[1:], lambda bi, i: (bi, 0, 0, 0, 0)),
                  kv_spec(C_KS), kv_spec(C_KS + 128), kv_spec(C_KW), kv_spec(C_KW + 128),
                  pl.BlockSpec(ovt.shape, lambda bi, i: (0, 0))],
        out_specs=pl.BlockSpec((None, tq, 512), lambda bi, i: (bi, i, 0)),
        out_shape=jax.ShapeDtypeStruct((b, t, 512), F32),
        scratch_shapes=[pltpu.VMEM((NSA_KV, t, 2 * DH), BF16), pltpu.VMEM((NSA_KV, t, 2 * DH), BF16),
                        pltpu.VMEM((2 * DH, t), BF16), pltpu.VMEM((2 * DH, t), BF16)],
        compiler_params=_cp(("arbitrary", "arbitrary")),
        name="nsa_prompt",
    )(z3, z3, kvcb, z3, z3, z3, z3, ovt)


def _nsa_sample_kernel(pt_ref, q_ref, gn_ref, kvs_ref, kvw_ref, win_ref, cmp_hbm, sel_hbm,
                       w1_ref, pe_ref, w2_ref, ov_ref, e_ref, o_ref,
                       cmp_buf, sel_buf, rows_scr, new_scr, sem, *, n_pages, past, t_new):
    b = pl.program_id(0)
    nb = pl.num_programs(0)
    slot = b % 2
    nch = past // CMP_STRIDE
    wb = win_ref.shape[-1]

    def page_copies(bb, sl, pg):
        toks = pl.ds(pg * PAGE, PAGE)
        p = pt_ref[bb, pg]
        return (pltpu.make_async_copy(cmp_hbm.at[p], cmp_buf.at[sl, :, :, toks], sem.at[0, sl]),
                pltpu.make_async_copy(sel_hbm.at[p], sel_buf.at[sl, :, :, toks], sem.at[1, sl]))

    def fetch(bb, sl):
        for pg in range(n_pages):
            for c in page_copies(bb, sl, pg):
                c.start()

    @pl.when(b == 0)
    def _():
        fetch(0, 0)

    @pl.when(b + 1 < nb)
    def _():
        fetch(b + 1, 1 - slot)

    for pg in range(n_pages):
        for c in page_copies(b, slot, pg):
            c.wait()

    new_scr[...] = jnp.zeros(new_scr.shape, F32)
    new_scr[0:SUB, 0:256] = kvs_ref[...]
    new_scr[0:SUB, 256:512] = kvw_ref[...]

    tblk = 512 if past % 512 == 0 else past

    def load(kv, tok):
        if tok == 0:
            for c0 in range(0, past, tblk):
                rows_scr[c0:c0 + tblk, :] = cmp_buf[slot, kv, :, c0:c0 + tblk].T
        return rows_scr[pl.ds(tok, nch, stride=CMP_STRIDE), :]

    kvcb = _compress(load, nch, w1_ref, pe_ref, w2_ref)

    tq = SUB
    n_slc = -(-(past + t_new) // SLC_BLOCK)
    q = q_ref[...]
    gn = gn_ref[...]
    npos = past + lax.broadcasted_iota(I32, (1, 128), 1)
    ppos = lax.broadcasted_iota(I32, (1, past), 1)
    wpos = past - wb + lax.broadcasted_iota(I32, (1, wb), 1)
    for g in range(NSA_KV):
        rows = slice(g * DH, (g + 1) * DH)
        q4 = _stack_heads(q, g)
        qpos, a_col, s_col = _head_cols(tq, past, g)
        o_cmp, sel = _cmp_branch(q4, kvcb[0][g], kvcb[1][g], ov_ref, qpos, a_col, s_col, tq, past, n_slc)
        sel4 = jnp.concatenate([sel[:, :128]] * NSA_GROUP, axis=0).astype(BF16)

        picked = _dot(sel4, e_ref[...]) > 0.5
        s_p = _bias_mask(_dot(q4, sel_buf[slot, 0, rows, :].astype(BF16)), ppos, picked & (ppos <= qpos), a_col, s_col)
        s_n = _masked_scores(q4, new_scr[:, rows].astype(BF16), npos, npos <= qpos, a_col, s_col)
        o_slc = _softmax_two(s_p, s_n, sel_buf[slot, 1, rows, :].astype(BF16),
                             new_scr[:, 128 + g * DH:128 + (g + 1) * DH].astype(BF16))

        s_p = _bias_mask(_dot(q4, win_ref[0, rows, :].astype(BF16)), wpos, (wpos <= qpos) & (wpos > qpos - WINDOW),
                         a_col, s_col)
        s_n = _masked_scores(q4, new_scr[:, 256 + g * DH:256 + (g + 1) * DH].astype(BF16), npos, npos <= qpos,
                             a_col, s_col)
        o_win = _softmax_two(s_p, s_n, win_ref[1, rows, :].astype(BF16),
                             new_scr[:, 384 + g * DH:384 + (g + 1) * DH].astype(BF16))

        o_ref[:, g * 256:(g + 1) * 256] = _gate_combine(gn, (o_cmp, o_slc, o_win), tq, g)


def _nsa_sample(z3, win, cache_cmp, cache_sel, page_table, w1t, pe_t, w2, ov, e_mat, t_new):
    bs = z3.shape[0]
    n_pages = page_table.shape[1]
    past = n_pages * PAGE
    wb = win.shape[-1]
    const = lambda shape: pl.BlockSpec(shape, lambda bi, pt, n=len(shape): (0,) * n)
    grid_spec = pltpu.PrefetchScalarGridSpec(
        num_scalar_prefetch=1,
        grid=(bs,),
        in_specs=[pl.BlockSpec((None, SUB, 512), lambda bi, pt: (bi, 0, C_Q // 512)),
                  pl.BlockSpec((None, SUB, 128), lambda bi, pt: (bi, 0, C_GN // 128)),
                  pl.BlockSpec((None, SUB, 256), lambda bi, pt: (bi, 0, C_KS // 256)),
                  pl.BlockSpec((None, SUB, 256), lambda bi, pt: (bi, 0, C_KW // 256)),
                  pl.BlockSpec((None, 2, 128, wb), lambda bi, pt: (bi, 0, 0, 0)),
                  pl.BlockSpec(memory_space=pl.ANY),
                  pl.BlockSpec(memory_space=pl.ANY),
                  const(w1t.shape), const(pe_t.shape), const(w2.shape), const(ov.shape), const(e_mat.shape)],
        out_specs=pl.BlockSpec((None, SUB, 512), lambda bi, pt: (bi, 0, 0)),
        scratch_shapes=[pltpu.VMEM((2, 2, 128, past), F32),
                        pltpu.VMEM((2, 2, 128, past), F32),
                        pltpu.VMEM((past, 128), F32),
                        pltpu.VMEM((128, 512), F32),
                        pltpu.SemaphoreType.DMA((2, 2))],
    )
    return pl.pallas_call(
        functools.partial(_nsa_sample_kernel, n_pages=n_pages, past=past, t_new=t_new),
        grid_spec=grid_spec,
        out_shape=jax.ShapeDtypeStruct((bs, SUB, 512), F32),
        compiler_params=_cp(("arbitrary",), 60 * 1024 * 1024),
        name="nsa_sample",
    )(page_table, z3, z3, z3, z3, win, cache_cmp, cache_sel, w1t, pe_t, w2, ov, e_mat)


def _hgrn_kernel(hq_ref, hf_ref, hi_ref, hg_ref, lb_ref, nw_ref, s0_ref, o_ref, sout_ref, st_scr,
                 *, tb, chunk, t_real, blk):
    j = pl.program_id(1)
    nw = nw_ref[...]
    row = lax.broadcasted_iota(I32, (chunk, 1), 0)
    real = row < t_real
    tri = (lax.broadcasted_iota(I32, (chunk, chunk), 0) >= lax.broadcasted_iota(I32, (chunk, chunk), 1)).astype(BF16)
    n_rows = min(chunk, t_real)

    @pl.when(j == 0)
    def _():
        for h in range(HG_HEADS):
            st_scr[h] = s0_ref[h].T

    def head_chunk(rows, h):
        cols = slice(h * HG_DK, (h + 1) * HG_DK)
        lb = lb_ref[h]
        q = _silu(hq_ref[rows, cols])
        f = lb + (1.0 - lb) * _sigmoid(hf_ref[rows, cols])
        k = jnp.where(real, 1.0 - f, 0.0)
        gl = jnp.where(real, jnp.log(f), 0.0)
        v = hi_ref[rows, cols]
        if chunk > 8:
            ga_, gb_, gc_ = _split3(gl)
            bcum = _dot(tri, ga_) + _dot(tri, gb_) + _dot(tri, gc_)
        else:
            bcum = jnp.zeros_like(gl)
            for s in range(t_real):
                bcum = bcum + jnp.where(row >= s, gl[s:s + 1, :], 0.0)
        st = st_scr[h]
        o_inter = _dot_nt((q * jnp.exp(bcum)).astype(BF16), st.astype(BF16))
        v16 = v.astype(BF16)
        parts = []
        for lo in range(0, chunk, blk):
            hi = min(lo + blk, chunk)
            acc = o_inter[lo:hi, :]
            if lo < n_rows:
                qj = q[lo:hi, :]
                bj = bcum[lo:hi, :]
                tj = row[lo:hi, :]
                for s in range(lo, min(hi, t_real)):
                    d = jnp.where(tj >= s, bj - bcum[s:s + 1, :], NEG)
                    w = jnp.sum(qj * jnp.exp(d) * k[s:s + 1, :], axis=-1, keepdims=True)
                    acc = acc + w * v[s:s + 1, :]
                if lo > 0:
                    ref = bcum[lo - 1:lo, :]
                    qt = (qj * jnp.exp(bj - ref)).astype(BF16)
                    kt = (k[0:lo, :] * jnp.exp(ref - bcum[0:lo, :])).astype(BF16)
                    acc = acc + _dot(_dot_nt(qt, kt).astype(BF16), v16[0:lo, :])
            parts.append(acc)
        o = parts[0] if len(parts) == 1 else jnp.concatenate(parts, axis=0)
        bc = bcum[chunk - 1:chunk, :]
        kt = k * jnp.exp(bc - bcum)
        st_scr[h] = st * jnp.exp(bc) + _dot_tn(v16, kt.astype(BF16))
        o = o * lax.rsqrt(jnp.mean(o * o, axis=-1, keepdims=True) + EPS) * nw
        o_ref[rows, cols] = o * _silu(hg_ref[rows, cols])

    def step(ci, carry):
        rows = pl.ds(pl.multiple_of(ci * chunk, chunk), chunk)
        for h in range(HG_HEADS):
            head_chunk(rows, h)
        return carry

    lax.fori_loop(0, tb // chunk, step, 0)

    @pl.when(j == pl.num_programs(1) - 1)
    def _():
        for h in range(HG_HEADS):
            sout_ref[h] = st_scr[h].T


def _hgrn(z3, lb, norm_w, s0, chunk, t_real, tb, blk):
    b, t, _ = z3.shape
    w = HG_HEADS * HG_DK
    col = lambda c0: pl.BlockSpec((None, tb, w), lambda bi, j, c=c0 // w: (bi, j, c))
    state = pl.BlockSpec((None, HG_HEADS, HG_DK, HG_DV), lambda bi, j: (bi, 0, 0, 0))
    return pl.pallas_call(
        functools.partial(_hgrn_kernel, tb=tb, chunk=chunk, t_real=t_real, blk=blk),
        grid=(b, t // tb),
        in_specs=[col(C_HQ), col(C_HF), col(C_HI), col(C_HG),
                  pl.BlockSpec((HG_HEADS, 1, HG_DK), lambda bi, j: (0, 0, 0)),
                  pl.BlockSpec((1, HG_DV), lambda bi, j: (0, 0)),
                  state],
        out_specs=[pl.BlockSpec((None, tb, w), lambda bi, j: (bi, j, 0)), state],
        out_shape=[jax.ShapeDtypeStruct((b, t, w), F32),
                   jax.ShapeDtypeStruct((b, HG_HEADS, HG_DK, HG_DV), F32)],
        scratch_shapes=[pltpu.VMEM((HG_HEADS, HG_DV, HG_DK), F32)],
        compiler_params=_cp(("arbitrary", "arbitrary")),
        name="hgrn2",
    )(z3, z3, z3, z3, lb.reshape(HG_HEADS, 1, HG_DK), norm_w.reshape(1, HG_DV), s0)


def _merge_kernel(x_ref, on_ref, oh_ref, ga_ref, gb_ref, wa_ref, wb_ref, wo_ref, g1_ref, sc_ref, sh_ref,
                  npost_ref, npre_ref, wr_ref, br_ref, y_ref, h_ref, ti_ref, tw_ref):
    a = _dot(on_ref[...].astype(BF16), wa_ref[...])
    bb = _dot(oh_ref[...].astype(BF16), wb_ref[...])
    m = _sigmoid(ga_ref[...]) * a + _sigmoid(gb_ref[...]) * bb
    out = _dot(m.astype(BF16), wo_ref[...])
    y = x_ref[...] + g1_ref[...] * _rms(out, npost_ref[...])
    y_ref[...] = y
    h = _rms(y, npre_ref[...]) * (1.0 + sc_ref[...]) + sh_ref[...]
    h_ref[...] = h
    hh, hl, _ = _split3(h)
    w = wr_ref[...]
    wh = w.astype(BF16)
    wl = (w - wh.astype(F32)).astype(BF16)
    logits = _dot(hh, wh) + _dot(hh, wl) + _dot(hl, wh) + br_ref[...]
    lane = lax.broadcasted_iota(I32, logits.shape, 1)
    lanef = lane.astype(F32)
    logits = jnp.where(lane < N_EXPERTS, logits, -jnp.inf)
    ti = jnp.zeros(logits.shape, F32)
    tv = jnp.zeros(logits.shape, F32)
    v0 = None
    for kk in range(TOP_K):
        mx = jnp.max(logits, axis=-1, keepdims=True)
        idx = jnp.min(jnp.where(logits == mx, lanef, 1e9), axis=-1, keepdims=True)
        v0 = mx if v0 is None else v0
        ti = jnp.where(lane == kk, idx, ti)
        tv = jnp.where(lane == kk, jnp.exp(mx - v0), tv)
        logits = jnp.where(lanef == idx, -jnp.inf, logits)
    ti_ref[...] = ti.astype(I32)
    tw_ref[...] = tv / jnp.sum(tv, axis=-1, keepdims=True)


def _merge(x2d, o_nsa, o_hg, z2d, g1, sc2, sh2, wa, wb, wo, n_post, n_pre, w_router, b_router, rows_per_mod, tm):
    n, d = x2d.shape
    row = lambda w: pl.BlockSpec((tm, w), lambda i: (i, 0))
    if rows_per_mod:
        per = rows_per_mod // tm
        mod_spec = pl.BlockSpec((None, 1, d), lambda i: (i // per, 0, 0))
    else:
        mod_spec = row(d)
    const2 = lambda a: pl.BlockSpec(a.shape, lambda i: (0, 0))
    wr = jnp.pad(w_router, ((0, 0), (0, 128 - N_EXPERTS)))
    br = jnp.pad(b_router, (0, 128 - N_EXPERTS)).reshape(1, 128)
    vec = lambda v: v.reshape(1, d)
    return pl.pallas_call(
        _merge_kernel,
        grid=(n // tm,),
        in_specs=[row(d), row(512), row(512),
                  pl.BlockSpec((tm, d), lambda i: (i, C_GA // 1024)),
                  pl.BlockSpec((tm, d), lambda i: (i, C_GB // 1024)),
                  const2(wa), const2(wb), const2(wo),
                  mod_spec, mod_spec, mod_spec,
                  pl.BlockSpec((1, d), lambda i: (0, 0)), pl.BlockSpec((1, d), lambda i: (0, 0)),
                  const2(wr), const2(br)],
        out_specs=[row(d), row(d), row(128), row(128)],
        out_shape=[jax.ShapeDtypeStruct((n, d), F32), jax.ShapeDtypeStruct((n, d), F32),
                   jax.ShapeDtypeStruct((n, 128), I32), jax.ShapeDtypeStruct((n, 128), F32)],
        compiler_params=_cp(("arbitrary",)),
        name="merge_router",
    )(x2d, o_nsa, o_hg, z2d, z2d, wa, wb, wo, g1, sc2, sh2, vec(n_post), vec(n_pre), wr, br)


def _moe_kernel(wt_ref, we_ref, lo_ref, hi_ref, ni_ref, tok_ref, tok_next_ref, rw_ref, h_hbm,
                wu_ref, bu_ref, wd_ref, bd_ref, o_ref, xbuf, sem, *, tm, n_tiles):
    w = pl.program_id(0)
    tile = wt_ref[w]
    first = (w == 0) | (tile != wt_ref[jnp.maximum(w - 1, 0)])
    valid = w < ni_ref[0]
    slot = tile % 2

    has_next = tile + 1 < n_tiles

    def row_copy(idx_ref, sl, r):
        return pltpu.make_async_copy(h_hbm.at[pl.ds(idx_ref[r], 1), :], xbuf.at[sl, pl.ds(r, 1), :], sem.at[sl])

    @pl.when(valid & (w == 0))
    def _():
        def body(r, c):
            row_copy(tok_ref, 0, r).start()
            return c
        lax.fori_loop(0, tm, body, 0, unroll=8)

    def ffn(issue_next, init):
        if init:
            pltpu.make_async_copy(h_hbm.at[pl.ds(0, tm), :], xbuf.at[slot], sem.at[slot]).wait()
        x = xbuf[slot].astype(BF16)
        if issue_next:
            for r in range(tm):
                row_copy(tok_next_ref, 1 - slot, r).start(priority=r % 2)
        z = _dot(x, wu_ref[...]) + bu_ref[...]
        de = wd_ref.shape[0]
        gate = jnp.minimum(z[:, :de], SWIGLU_LIMIT)
        up = jnp.clip(z[:, de:], -SWIGLU_LIMIT, SWIGLU_LIMIT)
        act = (up + 1.0) * gate * _sigmoid(SWIGLU_ALPHA * gate)
        y = _dot(act.astype(BF16), wd_ref[...]) + bd_ref[...]
        row = lax.broadcasted_iota(I32, (tm, 1), 0)
        mine = (row >= lo_ref[w]) & (row < hi_ref[w])
        contrib = jnp.where(mine, y * rw_ref[...], 0.0)
        o_ref[...] = contrib if init else o_ref[...] + contrib

    @pl.when(valid & first & has_next)
    def _():
        ffn(True, True)

    @pl.when(valid & first & jnp.logical_not(has_next))
    def _():
        ffn(False, True)

    @pl.when(valid & jnp.logical_not(first))
    def _():
        ffn(False, False)


def _moe_ffn(h2, items, row_token, row_weight, w_up, b_up, w_down, b_down, tm):
    n, d = h2.shape
    n_rows = row_token.shape[0]
    n_tiles = n_rows // tm
    ne, _, dh2 = w_up.shape
    wt, we, lo, hi, ni = items
    grid_spec = pltpu.PrefetchScalarGridSpec(
        num_scalar_prefetch=5,
        grid=(wt.shape[0],),
        in_specs=[pl.BlockSpec((tm,), lambda w, wt, *_: (wt[w],), memory_space=pltpu.SMEM),
                  pl.BlockSpec((tm,), lambda w, wt, *_: (jnp.minimum(wt[w] + 1, n_tiles - 1),), memory_space=pltpu.SMEM),
                  pl.BlockSpec((tm, 1), lambda w, wt, *_: (wt[w], 0)),
                  pl.BlockSpec(memory_space=pl.ANY),
                  pl.BlockSpec((None, d, dh2), lambda w, wt, we, *_: (we[w], 0, 0)),
                  pl.BlockSpec((None, 1, dh2), lambda w, wt, we, *_: (we[w], 0, 0)),
                  pl.BlockSpec((None, dh2 // 2, d), lambda w, wt, we, *_: (we[w], 0, 0)),
                  pl.BlockSpec((None, 1, d), lambda w, wt, we, *_: (we[w], 0, 0))],
        out_specs=pl.BlockSpec((tm, d), lambda w, wt, *_: (wt[w], 0)),
        scratch_shapes=[pltpu.VMEM((2, tm, d), F32), pltpu.SemaphoreType.DMA((2,))],
    )
    return pl.pallas_call(
        functools.partial(_moe_kernel, tm=tm, n_tiles=n_tiles),
        grid_spec=grid_spec,
        out_shape=jax.ShapeDtypeStruct((n_rows, d), F32),
        compiler_params=_cp(("arbitrary",)),
        name="moe_ffn",
    )(wt, we, lo, hi, ni, row_token, row_token, row_weight, h2, w_up, b_up.reshape(ne, 1, dh2),
      w_down, b_down.reshape(ne, 1, d))


def _combine_kernel(pos_ref, y_hbm, y1_ref, g2_ref, npost_ref, o_ref, buf, sem, *, tm):
    def body(r, c):
        for kk in range(TOP_K):
            pltpu.make_async_copy(y_hbm.at[pl.ds(pos_ref[r * TOP_K + kk], 1), :], buf.at[kk, pl.ds(r, 1), :],
                                  sem.at[0]).start(priority=kk % 2)
        return c
    lax.fori_loop(0, tm, body, 0, unroll=4)
    for kk in range(TOP_K):
        pltpu.make_async_copy(y_hbm.at[pl.ds(0, tm), :], buf.at[kk], sem.at[0]).wait()
    moe = (buf[0] + buf[1]) + (buf[2] + buf[3])
    o_ref[...] = y1_ref[...] + g2_ref[...] * _rms(moe, npost_ref[...])


def _combine(pos_flat, y_sorted, y1, g2, n_post, rows_per_mod, tm):
    n, d = y1.shape
    if rows_per_mod:
        per = rows_per_mod // tm
        mod_spec = pl.BlockSpec((None, 1, d), lambda i: (i // per, 0, 0))
    else:
        mod_spec = pl.BlockSpec((tm, d), lambda i: (i, 0))
    return pl.pallas_call(
        functools.partial(_combine_kernel, tm=tm),
        grid=(n // tm,),
        in_specs=[pl.BlockSpec((tm * TOP_K,), lambda i: (i,), memory_space=pltpu.SMEM),
                  pl.BlockSpec(memory_space=pl.ANY),
                  pl.BlockSpec((tm, d), lambda i: (i, 0)),
                  mod_spec,
                  pl.BlockSpec((1, d), lambda i: (0, 0))],
        out_specs=pl.BlockSpec((tm, d), lambda i: (i, 0)),
        out_shape=jax.ShapeDtypeStruct((n, d), F32),
        scratch_shapes=[pltpu.VMEM((TOP_K, tm, d), F32), pltpu.SemaphoreType.DMA((1,))],
        compiler_params=_cp(("arbitrary",)),
        name="moe_combine",
    )(pos_flat, y_sorted, y1, g2, n_post.reshape(1, d))


def _route_tables(top_i, top_w, tm):
    m = top_i.shape[0] * TOP_K
    m_pad = -(-m // tm) * tm
    n_tiles = m_pad // tm
    e_flat = top_i.reshape(-1).astype(I32)
    iota = jnp.arange(m, dtype=I32)
    _, order, w_sorted = lax.sort((e_flat, iota, top_w.reshape(-1)), num_keys=1, is_stable=True)
    _, pos_of = lax.sort((order, iota), num_keys=1)
    row_token = jnp.pad(order // TOP_K, (0, m_pad - m))
    row_weight = jnp.pad(w_sorted, (0, m_pad - m)).reshape(m_pad, 1)
    ex = jnp.arange(N_EXPERTS, dtype=I32)
    counts = jnp.sum((e_flat[:, None] == ex[None, :]).astype(I32), axis=0)
    uend = jnp.cumsum(counts)
    ustart = uend - counts
    first_tile = ustart // tm
    n_item_e = jnp.where(counts > 0, (uend - 1) // tm - first_tile + 1, 0)
    iend = jnp.cumsum(n_item_e)
    n_items = iend[-1]
    wid = jnp.arange(n_tiles + N_EXPERTS - 1, dtype=I32)
    we = jnp.minimum(jnp.sum((wid[:, None] >= iend[None, :]).astype(I32), axis=1), N_EXPERTS - 1)
    onehot = (we[:, None] == ex[None, :]).astype(I32)
    pick = lambda v: jnp.sum(onehot * v[None, :], axis=1)
    wt = pick(first_tile) + (wid - pick(iend - n_item_e))
    lo = jnp.maximum(pick(ustart), wt * tm) - wt * tm
    hi = jnp.minimum(pick(uend), (wt + 1) * tm) - wt * tm
    live = wid < n_items
    last_e = jnp.max(jnp.where(counts > 0, ex, 0))
    items = (jnp.where(live, wt, n_tiles - 1), jnp.where(live, we, last_e),
             jnp.where(live, lo, 0), jnp.where(live, hi, 0), n_items.reshape(1))
    return tuple(a.astype(I32) for a in items), row_token, row_weight, pos_of


def _overlap_table(n_rows, n_cmp, n_cols):
    start = np.arange(n_rows)[:, None] * CMP_STRIDE
    j0 = np.arange(n_cols)[None, :] * SLC_BLOCK
    ov = (start < j0 + SLC_BLOCK) & (start + CMP_BLOCK > j0) & (np.arange(n_rows)[:, None] < n_cmp)
    return jnp.asarray(ov.astype(np.float32), dtype=BF16)


def _expand_table(n_rows, n_keys):
    e = (np.arange(n_keys)[None, :] // SLC_BLOCK) == np.arange(n_rows)[:, None]
    return jnp.asarray(e.astype(np.float32), dtype=BF16)


def _pack_w_in(w_in):
    d = w_in.shape[0]
    q = w_in[:, 0:512]
    kv = w_in[:, 512:1280]
    gn = w_in[:, 1280:1304]
    hh = w_in[:, 1304:3352]
    ga = w_in[:, 3352:4376]
    gb = w_in[:, 4376:5400]
    z = lambda w: jnp.zeros((d, w), w_in.dtype)
    return jnp.concatenate([ga, gb, q, hh, kv, gn, z(ZW - C_GN - 24)], axis=1).astype(BF16)


def _pack_compress(cmp_pe, cmp_w1, cmp_w2):
    r = CMP_BLOCK // CMP_STRIDE
    wt = cmp_w1.reshape(2, r, CMP_STRIDE, DH, CMP_HIDDEN).transpose(0, 2, 3, 1, 4).reshape(2, CMP_STRIDE, DH, r * CMP_HIDDEN)
    zero = jnp.zeros_like(wt)
    w1bd = jnp.concatenate([jnp.concatenate([wt, zero], axis=-1), jnp.concatenate([zero, wt], axis=-1)], axis=2)
    w1bd = w1bd.reshape(2, CMP_STRIDE * 2 * DH, 2 * r * CMP_HIDDEN)
    pe_t = jnp.pad(cmp_pe.reshape(2, r, CMP_STRIDE, DH), ((0, 0), (0, SUB - r), (0, 0), (0, DH)))
    return w1bd.astype(BF16), pe_t.reshape(2, SUB, CMP_STRIDE * 2 * DH), cmp_w2


def _kv_out(z3, c0, rows):
    b = z3.shape[0]
    return z3[:, rows, c0:c0 + 256].reshape(b, -1, 2, NSA_KV, DH)[None]


def kernel(x_prompt, x_sample, c_prompt, c_sample, cache_cmp, cache_sel, state_win, state_hgrn, page_table, w_ada, b_ada, norm_mix_pre, norm_mix_post, norm_ffn_pre, norm_ffn_post, w_in, cmp_pe, cmp_w1, cmp_w2, hg_lb_logits, hg_norm, w_branch_a, w_branch_b, w_out, w_router, b_router, w_up, b_up, w_down, b_down):
    bp, t, d = x_prompt.shape
    bs, ts, _ = x_sample.shape
    depth = w_in.shape[0]
    assert depth == 1 and ts <= SUB
    n_pool = cache_cmp.shape[1]
    past = page_table.shape[1] * PAGE
    lb_all = jnp.cumsum(jax.nn.softmax(hg_lb_logits.astype(F32), axis=0), axis=0)

    l = 0
    w_packed = _pack_w_in(w_in[l])
    w1t, pe_t, w2c = _pack_compress(cmp_pe[l], cmp_w1[l], cmp_w2[l])
    wa, wb, wo = w_branch_a[l].astype(BF16), w_branch_b[l].astype(BF16), w_out[l].astype(BF16)
    wu, wd = w_up[l].astype(BF16), w_down[l].astype(BF16)

    mod = _modulation(jnp.concatenate([c_prompt, c_sample], axis=0), w_ada[l], b_ada[l])
    mods = [m[:, None, :] for m in jnp.split(mod, 6, axis=-1)]
    sh1, sc1, g1, sh2, sc2, g2 = mods
    p_, s_ = slice(0, bp), slice(bp, bp + bs)

    tm_p = 256 if t % 256 == 0 else t
    zp = _inproj(x_prompt.reshape(bp * t, d), norm_mix_pre[l], sc1[p_], sh1[p_], w_packed, t, tm_p)
    zp3 = zp.reshape(bp, t, ZW)
    nch = t // CMP_STRIDE
    n_slc = -(-t // SLC_BLOCK)
    kvcb_p = _compress_prompt(zp3, w1t, pe_t, w2c)
    tq = 256 if t % 256 == 0 else t
    o_nsa_p = _nsa_prompt(zp3, kvcb_p, _overlap_table(nch, nch - 1, -(-n_slc // SUB) * SUB).T, tq)
    s0_p = jnp.zeros((bp, HG_HEADS, HG_DK, HG_DV), F32)
    chunk = int(np.gcd(t, HG_CHUNK))
    tb_h = 512 if t % 512 == 0 else t
    o_hg_p, hg_state_p = _hgrn(zp3, lb_all[l], hg_norm[l], s0_p, chunk, chunk, tb_h, 16 if chunk % 16 == 0 else chunk)
    tm_m = 512 if t % 512 == 0 else t
    y1_p, h2_p, ti_p, tw_p = _merge(x_prompt.reshape(bp * t, d), o_nsa_p.reshape(bp * t, 512), o_hg_p.reshape(bp * t, 512),
                                    zp, g1[p_], sc2[p_], sh2[p_], wa, wb, wo, norm_mix_post[l], norm_ffn_pre[l],
                                    w_router[l], b_router[l], t, tm_m)

    xs = jnp.pad(x_sample, ((0, 0), (0, SUB - ts), (0, 0))).reshape(bs * SUB, d)
    rep = lambda m: jnp.broadcast_to(m[s_], (bs, SUB, d)).reshape(bs * SUB, d)
    tm_s = 512 if (bs * SUB) % 512 == 0 else bs * SUB
    zs = _inproj(xs, norm_mix_pre[l], rep(sc1), rep(sh1), w_packed, 0, min(tm_s, 256))
    zs3 = zs.reshape(bs, SUB, ZW)
    tok_minor = lambda a: jnp.transpose(a, (0, 2, 3, 4, 1)).reshape(a.shape[0], 2, NSA_KV * DH, a.shape[1])
    o_nsa_s = _nsa_sample(zs3, tok_minor(state_win[l]), tok_minor(cache_cmp[l]), tok_minor(cache_sel[l]),
                          page_table, w1t, pe_t, w2c,
                          _overlap_table(past // CMP_STRIDE, past // CMP_STRIDE - 1, 256),
                          _expand_table(128, past), ts)
    o_hg_s, hg_state_s = _hgrn(zs3, lb_all[l], hg_norm[l], state_hgrn[l], SUB, ts, SUB, SUB)
    y1_s, h2_s, ti_s, tw_s = _merge(xs, o_nsa_s.reshape(bs * SUB, 512), o_hg_s.reshape(bs * SUB, 512), zs,
                                    rep(g1), rep(sc2), rep(sh2), wa, wb, wo, norm_mix_post[l], norm_ffn_pre[l],
                                    w_router[l], b_router[l], 0, tm_s)
    real = lambda a: a.reshape(bs, SUB, -1)[:, :ts].reshape(bs * ts, -1)

    h2 = jnp.concatenate([h2_p, real(h2_s)], axis=0)
    top_i = jnp.concatenate([ti_p[:, :TOP_K], real(ti_s)[:, :TOP_K]], axis=0)
    top_w = jnp.concatenate([tw_p[:, :TOP_K], real(tw_s)[:, :TOP_K]], axis=0)
    tm_e = 512
    items, row_token, row_weight, pos_of = _route_tables(top_i, top_w, tm_e)
    y_sorted = _moe_ffn(h2, items, row_token, row_weight, wu, b_up[l], wd, b_down[l], tm_e)
    np_tok = bp * t
    tm_cp = 256 if t % 256 == 0 else t
    y_p = _combine(pos_of[:np_tok * TOP_K], y_sorted, y1_p, g2[p_], norm_ffn_post[l], t, tm_cp).reshape(bp, t, d)
    g2_s = jnp.broadcast_to(g2[s_], (bs, ts, d)).reshape(bs * ts, d)
    tm_cs = 256 if (bs * ts) % 256 == 0 else bs * ts
    y_s = _combine(pos_of[np_tok * TOP_K:], y_sorted, real(y1_s), g2_s, norm_ffn_post[l], 0, tm_cs).reshape(bs, ts, d)
    keep = min(WINDOW, t)
    new_rows = zs3[:, :ts, C_KW:C_KW + 256].reshape(bs, ts, 2, NSA_KV, DH)
    win_s = jnp.concatenate([state_win[l], new_rows], axis=1)[:, ts:][None]
    return (y_p, y_s,
            _kv_out(zp3, C_KC, slice(0, t)), _kv_out(zs3, C_KC, slice(0, ts)),
            _kv_out(zp3, C_KS, slice(0, t)), _kv_out(zs3, C_KS, slice(0, ts)),
            _kv_out(zp3, C_KW, slice(t - keep, t)), win_s,
            hg_state_p[None], hg_state_s[None])
```

```python
import functools

import numpy as np
import jax
import jax.numpy as jnp
from jax import lax
from jax.experimental import pallas as pl
from jax.experimental.pallas import tpu as pltpu

F32, BF16, I32 = jnp.float32, jnp.bfloat16, jnp.int32

NSA_HEADS, NSA_KV, NSA_GROUP, DH = 8, 2, 4, 64
CMP_BLOCK, CMP_STRIDE, CMP_HIDDEN = 32, 16, 128
SLC_BLOCK, SLC_TOPK, WINDOW = 64, 16, 512
HG_HEADS, HG_DK, HG_DV, HG_CHUNK = 4, 128, 128, 64
N_EXPERTS, TOP_K = 32, 4
SWIGLU_LIMIT, SWIGLU_ALPHA = 7.0, 1.702
EPS, NEG, FORCE = 1e-6, -1e30, 1e4
PAGE = 128

C_GA, C_GB, C_Q, C_HQ, C_HF, C_HI, C_HG = 0, 1024, 2048, 2560, 3072, 3584, 4096
C_KC, C_KS, C_KW, C_GN, ZW = 4608, 4864, 5120, 5376, 5632
SUB = 8

VMEM_LIMIT = 56 * 1024 * 1024


def _cp(sem, vmem=VMEM_LIMIT):
    return pltpu.CompilerParams(dimension_semantics=sem, vmem_limit_bytes=vmem)


def _dot(a, b):
    return jnp.dot(a, b, preferred_element_type=F32)


def _dot_nt(a, b):
    return lax.dot_general(a, b, (((1,), (1,)), ((), ())), preferred_element_type=F32)


def _dot_tn(a, b):
    return lax.dot_general(a, b, (((0,), (0,)), ((), ())), preferred_element_type=F32)


def _split3(x):
    a = x.astype(BF16)
    r = x - a.astype(F32)
    b = r.astype(BF16)
    c = (r - b.astype(F32)).astype(BF16)
    return a, b, c


def _dot3(x, w_bf16):
    a, b, c = _split3(x)
    return _dot(a, w_bf16) + _dot(b, w_bf16) + _dot(c, w_bf16)


def _sigmoid(x):
    return 1.0 / (1.0 + jnp.exp(-x))


def _silu(x):
    return x * _sigmoid(x)


def _rms(x, g):
    return x * lax.rsqrt(jnp.mean(x * x, axis=-1, keepdims=True) + EPS) * g


def _gelu_tanh(x):
    return 0.5 * x * (1.0 + jnp.tanh(0.7978845608028654 * (x + 0.044715 * (x * x * x))))


def _mod_kernel(c_ref, w_ref, b_ref, o_ref):
    o_ref[...] = _dot(_silu(c_ref[...]).astype(BF16), w_ref[...].astype(BF16)) + b_ref[...]


def _modulation(c, w_ada, b_ada):
    m, d = c.shape
    nw = w_ada.shape[1]
    tn = nw // 6
    return pl.pallas_call(
        _mod_kernel,
        grid=(nw // tn,),
        in_specs=[pl.BlockSpec((m, d), lambda j: (0, 0)),
                  pl.BlockSpec((d, tn), lambda j: (0, j)),
                  pl.BlockSpec((1, tn), lambda j: (0, j))],
        out_specs=pl.BlockSpec((m, tn), lambda j: (0, j)),
        out_shape=jax.ShapeDtypeStruct((m, nw), F32),
        compiler_params=_cp(("arbitrary",)),
        name="modulation",
    )(c, w_ada, b_ada.reshape(1, nw))


INPROJ_TN = 512


def _inproj_kernel(x_ref, g_ref, sc_ref, sh_ref, w_ref, o_ref):
    h = (_rms(x_ref[...], g_ref[...]) * (1.0 + sc_ref[...]) + sh_ref[...]).astype(BF16)
    for c0 in range(0, o_ref.shape[1], INPROJ_TN):
        o_ref[:, c0:c0 + INPROJ_TN] = _dot(h, w_ref[:, c0:c0 + INPROJ_TN])


def _inproj(x2d, g, sc, sh, w_packed, rows_per_mod, tm):
    n, d = x2d.shape
    zw = w_packed.shape[1]
    if rows_per_mod:
        per = rows_per_mod // tm
        mod_spec = pl.BlockSpec((None, 1, d), lambda i: (i // per, 0, 0))
    else:
        mod_spec = pl.BlockSpec((tm, d), lambda i: (i, 0))
    return pl.pallas_call(
        _inproj_kernel,
        grid=(n // tm,),
        in_specs=[pl.BlockSpec((tm, d), lambda i: (i, 0)),
                  pl.BlockSpec((1, d), lambda i: (0, 0)),
                  mod_spec, mod_spec,
                  pl.BlockSpec((d, zw), lambda i: (0, 0))],
        out_specs=pl.BlockSpec((tm, zw), lambda i: (i, 0)),
        out_shape=jax.ShapeDtypeStruct((n, zw), F32),
        compiler_params=_cp(("arbitrary",)),
        name="inproj",
    )(x2d, g.reshape(1, d), sc, sh, w_packed)


def _compress(load, nch, w1_ref, pe_ref, w2_ref):
    out = []
    for kv in range(2):
        w1 = w1_ref[kv]
        x = jnp.concatenate([load(kv, tok) for tok in range(CMP_STRIDE)], axis=1).astype(BF16)
        acc = _dot(x, w1)
        pe = _dot(pe_ref[kv].astype(BF16), w1)
        pre0 = pe[0:1, :CMP_HIDDEN] + pe[1:2, CMP_HIDDEN:2 * CMP_HIDDEN]
        w2 = w2_ref[kv].astype(BF16)
        row = []
        for g in range(2):
            c0 = g * 2 * CMP_HIDDEN
            pre = (acc[:, c0:c0 + CMP_HIDDEN]
                   + pltpu.roll(acc[:, c0 + CMP_HIDDEN:c0 + 2 * CMP_HIDDEN], nch - 1, 0) + pre0)
            row.append(_dot(_gelu_tanh(pre).astype(BF16), w2))
        out.append(row)
    return out


def _compress_prompt_kernel(k_ref, v_ref, w1_ref, pe_ref, w2_ref, o_ref, *, nch):
    def load(kv, tok):
        return (k_ref, v_ref)[kv][pl.ds(tok, nch, stride=CMP_STRIDE), :]

    out = _compress(load, nch, w1_ref, pe_ref, w2_ref)
    for kv in range(2):
        for g in range(2):
            o_ref[kv, g] = out[kv][g]


def _compress_prompt(z3, w1t, pe_t, w2):
    b, t, _ = z3.shape
    nch = t // CMP_STRIDE
    return pl.pallas_call(
        functools.partial(_compress_prompt_kernel, nch=nch),
        grid=(b,),
        in_specs=[pl.BlockSpec((None, t, 128), lambda i: (i, 0, C_KC // 128)),
                  pl.BlockSpec((None, t, 128), lambda i: (i, 0, C_KC // 128 + 1)),
                  pl.BlockSpec(w1t.shape, lambda i: (0, 0, 0)),
                  pl.BlockSpec(pe_t.shape, lambda i: (0, 0, 0)),
                  pl.BlockSpec(w2.shape, lambda i: (0, 0, 0))],
        out_specs=pl.BlockSpec((None, 2, 2, nch, DH), lambda i: (i, 0, 0, 0, 0)),
        out_shape=jax.ShapeDtypeStruct((b, 2, 2, nch, DH), F32),
        compiler_params=_cp(("arbitrary",)),
        name="compress_prompt",
    )(z3, z3, w1t, pe_t, w2)


def _slope(g, r):
    return 2.0 ** (-(g * NSA_GROUP + r + 1))


def _stack_heads(q, g):
    parts = [q[:, (g * NSA_GROUP + r) * DH:(g * NSA_GROUP + r + 1) * DH] for r in range(NSA_GROUP)]
    return (jnp.concatenate(parts, axis=0) * (DH ** -0.5)).astype(BF16)


def _head_cols(tq, q0, g):
    pos1 = q0 + lax.broadcasted_iota(I32, (tq, 1), 0)
    qpos = jnp.concatenate([pos1] * NSA_GROUP, axis=0)
    slope = jnp.concatenate([jnp.full((tq, 1), _slope(g, r), F32) for r in range(NSA_GROUP)], axis=0)
    return qpos, -slope * qpos.astype(F32), slope


def _bias_mask(s, kpos_row, valid, a_col, s_col):
    return jnp.where(valid, s + (a_col + s_col * kpos_row.astype(F32)), NEG)


def _masked_scores(q4, k_bf16, kpos_row, valid, a_col, s_col):
    return _bias_mask(_dot_nt(q4, k_bf16), kpos_row, valid, a_col, s_col)


def _softmax_two(s_p, s_n, vt_p, v_n):
    m = jnp.maximum(jnp.max(s_p, axis=-1, keepdims=True), jnp.max(s_n, axis=-1, keepdims=True))
    p_p = jnp.exp(s_p - m)
    p_n = jnp.exp(s_n - m)
    l = jnp.sum(p_p, axis=-1, keepdims=True) + jnp.sum(p_n, axis=-1, keepdims=True)
    return (_dot_nt(p_p.astype(BF16), vt_p) + _dot(p_n.astype(BF16), v_n)) / l


def _topk_mask(score, k, n):
    lane = lax.broadcasted_iota(I32, score.shape, 1)
    ahead = jnp.zeros(score.shape, F32)
    for i in range(n):
        col = score[:, i:i + 1]
        ahead = ahead + jnp.where((col > score) | ((col == score) & (lane > i)), 1.0, 0.0)
    return jnp.where((ahead < k) & (lane < n), 1.0, 0.0)


def _cmp_attend(q4, kcb, vcb, qpos, a_col, s_col, tq):
    ncmp = kcb.shape[0]
    endpos = lax.broadcasted_iota(I32, (1, ncmp), 1) * CMP_STRIDE + (CMP_BLOCK - 1)
    valid = endpos <= qpos
    s = _masked_scores(q4, kcb.astype(BF16), endpos, valid, a_col, s_col)
    m = jnp.max(s, axis=-1, keepdims=True)
    e = jnp.exp(s - m)
    p = jnp.where(valid, e, 0.0) / jnp.sum(e, axis=-1, keepdims=True)
    o = _dot(p.astype(BF16), vcb.astype(BF16))
    return o, p[0:tq] + p[tq:2 * tq] + p[2 * tq:3 * tq] + p[3 * tq:4 * tq]


def _cmp_branch(q4, kcb, vcb, ov_ref, qpos, a_col, s_col, tq, q0, n_slc):
    o, psum = _cmp_attend(q4, kcb, vcb, qpos, a_col, s_col, tq)
    imp = _dot3(psum, ov_ref[...])
    jl = lax.broadcasted_iota(I32, imp.shape, 1)
    cur = lax.shift_right_logical(q0 + lax.broadcasted_iota(I32, (tq, 1), 0), 6)
    forced = (jl == 0) | (jl == cur) | (jl == cur - 1)
    score = jnp.where(forced, FORCE, jnp.where(jl <= cur, imp, -FORCE))
    return o, _topk_mask(score, min(SLC_TOPK, n_slc), n_slc)


def _select_cols(imp, tq, q0, n_slc):
    jr = lax.broadcasted_iota(I32, imp.shape, 0)
    cur = lax.shift_right_logical(q0 + lax.broadcasted_iota(I32, (1, tq), 1), 6)
    forced = (jr == 0) | (jr == cur) | (jr == cur - 1)
    score = jnp.where(forced, FORCE, jnp.where(jr <= cur, imp, -FORCE))
    ahead = jnp.zeros(score.shape, F32)
    for i in range(n_slc):
        rw = score[i:i + 1, :]
        ahead = ahead + jnp.where((rw > score) | ((rw == score) & (jr > i)), 1.0, 0.0)
    return jnp.where((ahead < min(SLC_TOPK, n_slc)) & (jr < n_slc), 1.0, 0.0)


def _online(s, carry, v16):
    m, l, acc = carry
    m_new = jnp.maximum(m, jnp.max(s, axis=-1, keepdims=True))
    alpha = jnp.exp(m - m_new)
    p = jnp.exp(s - m_new)
    return m_new, alpha * l + jnp.sum(p, axis=-1, keepdims=True), alpha * acc + _dot(p.astype(BF16), v16)


def _gate_combine(gn, outs, tq, g):
    sig = _sigmoid(gn)
    heads = []
    for r in range(NSA_GROUP):
        acc = None
        for br, o in enumerate(outs):
            c = br * NSA_HEADS + g * NSA_GROUP + r
            t = sig[:, c:c + 1] * o[r * tq:(r + 1) * tq, :]
            acc = t if acc is None else acc + t
        heads.append(acc)
    return jnp.concatenate(heads, axis=1)


MASK_BIG = 2.0 ** 60
N_SEL_COLS = 32
COL_POS_HI, COL_POS_LO = 32, 33


def _nsa_prompt_kernel(q_ref, gn_ref, kvc_ref, ks_ref, vs_ref, kw_ref, vw_ref, ovt_ref, o_ref,
                       ksa_scr, kwa_scr, vst_scr, vwt_scr, *, tq, t_len, n_slc):
    i = pl.program_id(1)
    q0 = i * tq
    nr = NSA_GROUP * tq

    @pl.when(i == 0)
    def _():
        kp = lax.broadcasted_iota(I32, (t_len, DH), 0)
        ln = lax.broadcasted_iota(I32, (t_len, DH), 1)
        pos_cols = jnp.where(ln == COL_POS_HI, lax.shift_right_logical(kp, 7).astype(F32),
                             jnp.where(ln == COL_POS_LO, (kp & 127).astype(F32), 0.0))
        sel_cols = jnp.where(ln == lax.shift_right_logical(kp, 6), 1.0, pos_cols)
        for g in range(NSA_KV):
            lanes = slice(g * DH, (g + 1) * DH)
            ksa_scr[g] = jnp.concatenate([ks_ref[:, lanes], sel_cols], axis=1).astype(BF16)
            kwa_scr[g] = jnp.concatenate([kw_ref[:, lanes], pos_cols], axis=1).astype(BF16)
        for c0 in range(0, t_len, tq):
            vst_scr[:, c0:c0 + tq] = vs_ref[c0:c0 + tq, :].T.astype(BF16)
            vwt_scr[:, c0:c0 + tq] = vw_ref[c0:c0 + tq, :].T.astype(BF16)

    lane = lax.broadcasted_iota(I32, (1, nr), 1)
    qpos = q0 + (lane & (tq - 1))
    head = lax.shift_right_logical(lane, tq.bit_length() - 1)
    krow = lax.broadcasted_iota(I32, (tq, 1), 0)
    sig_t = _sigmoid(gn_ref[...].T)
    init = (jnp.full((1, nr), NEG, F32), jnp.zeros((1, nr), F32), jnp.zeros((DH, nr), F32))
    qas, o_cmps = [], []
    for g in range(NSA_KV):
        slope = jnp.where(head == 0, _slope(g, 0), jnp.where(head == 1, _slope(g, 1),
                          jnp.where(head == 2, _slope(g, 2), _slope(g, 3))))
        qt = q_ref[:, g * 256:(g + 1) * 256].T * (DH ** -0.5)
        q4t = jnp.concatenate([qt[r * DH:(r + 1) * DH, :] for r in range(NSA_GROUP)], axis=1)

        ncmp = kvc_ref.shape[2]
        endpos = lax.broadcasted_iota(I32, (ncmp, 1), 0) * CMP_STRIDE + (CMP_BLOCK - 1)
        valid = endpos <= qpos
        s = _dot(kvc_ref[0, g].astype(BF16), q4t.astype(BF16)) + slope * endpos.astype(F32)
        s = jnp.where(valid, s, NEG)
        e = jnp.exp(s - jnp.max(s, axis=0, keepdims=True))
        p = jnp.where(valid, e, 0.0) / jnp.sum(e, axis=0, keepdims=True)
        o_cmp = _dot_tn(kvc_ref[1, g].astype(BF16), p.astype(BF16))
        psum_t = p[:, 0:tq] + p[:, tq:2 * tq] + p[:, 2 * tq:3 * tq] + p[:, 3 * tq:4 * tq]
        imp_t = sum(_dot(ovt_ref[...], part) for part in _split3(psum_t))
        sel_t = _select_cols(imp_t, tq, q0, n_slc)
        sel4 = jnp.concatenate([sel_t] * NSA_GROUP, axis=1)
        jr = lax.broadcasted_iota(I32, (N_SEL_COLS, nr), 0)
        pos_rows = jnp.where(jr == COL_POS_HI - N_SEL_COLS, 128.0 * slope,
                             jnp.where(jr == COL_POS_LO - N_SEL_COLS, slope, 0.0))
        pieces = [q4t, (sel4 - 1.0) * MASK_BIG]
        if sel4.shape[0] < N_SEL_COLS:
            pieces.append(jnp.zeros((N_SEL_COLS - sel4.shape[0], nr), F32))
        qas.append(jnp.concatenate(pieces + [pos_rows], axis=0).astype(BF16))
        o_cmps.append(o_cmp)

    def attend(ka_ref, vt_ref, kt_lo, band):
        def tile(kt, carries, diag):
            k0 = pl.multiple_of(kt * tq, tq)
            out = []
            for g in range(NSA_KV):
                s = _dot(ka_ref[g, pl.ds(k0, tq), :], qas[g])
                if band:
                    s = jnp.where(k0 + krow > qpos - WINDOW, s, NEG)
                if diag:
                    s = jnp.where(k0 + krow <= qpos, s, NEG)
                m, l, acc = carries[g]
                m_new = jnp.maximum(m, jnp.max(s, axis=0, keepdims=True))
                alpha = jnp.exp(m - m_new)
                pt = jnp.exp(s - m_new)
                l = alpha * l + jnp.sum(pt, axis=0, keepdims=True)
                acc = alpha * acc + _dot(vt_ref[g * DH:(g + 1) * DH, pl.ds(k0, tq)], pt.astype(BF16))
                out.append((m_new, l, acc))
            return tuple(out)

        carries = lax.fori_loop(kt_lo, i, lambda kt, c: tile(kt, c, False), (init,) * NSA_KV)
        return [acc / l for _, l, acc in tile(i, carries, True)]

    o_slcs = attend(ksa_scr, vst_scr, 0, False)
    o_wins = attend(kwa_scr, vwt_scr, jnp.maximum(i - WINDOW // tq, 0), True)

    for g in range(NSA_KV):
        o_cmp, o_slc, o_win = o_cmps[g], o_slcs[g], o_wins[g]
        heads = []
        for r in range(NSA_GROUP):
            cols = slice(r * tq, (r + 1) * tq)
            acc = None
            for br, o in enumerate((o_cmp, o_slc, o_win)):
                c = br * NSA_HEADS + g * NSA_GROUP + r
                t = sig_t[c:c + 1, :] * o[:, cols]
                acc = t if acc is None else acc + t
            heads.append(acc)
        o_ref[:, g * 256:(g + 1) * 256] = jnp.concatenate(heads, axis=0).T


def _nsa_prompt(z3, kvcb, ovt, tq):
    b, t, _ = z3.shape
    n_slc = -(-t // SLC_BLOCK)
    assert n_slc <= N_SEL_COLS and WINDOW % tq == 0 and t % tq == 0 and tq & (tq - 1) == 0
    kv_spec = lambda col: pl.BlockSpec((None, t, 128), lambda bi, i, c=col // 128: (bi, 0, c))
    return pl.pallas_call(
        functools.partial(_nsa_prompt_kernel, tq=tq, t_len=t, n_slc=n_slc),
        grid=(b, t // tq),
        in_specs=[pl.BlockSpec((None, tq, 512), lambda bi, i: (bi, i, C_Q // 512)),
                  pl.BlockSpec((None, tq, 128), lambda bi, i: (bi, i, C_GN // 128)),
                  pl.BlockSpec((None,) + kvcb.shape[1:], lambda bi, i: (bi, 0, 0, 0, 0)),
                  kv_spec(C_KS), kv_spec(C_KS + 128), kv_spec(C_KW), kv_spec(C_KW + 128),
                  pl.BlockSpec(ovt.shape, lambda bi, i: (0, 0))],
        out_specs=pl.BlockSpec((None, tq, 512), lambda bi, i: (bi, i, 0)),
        out_shape=jax.ShapeDtypeStruct((b, t, 512), F32),
        scratch_shapes=[pltpu.VMEM((NSA_KV, t, 2 * DH), BF16), pltpu.VMEM((NSA_KV, t, 2 * DH), BF16),
                        pltpu.VMEM((2 * DH, t), BF16), pltpu.VMEM((2 * DH, t), BF16)],
        compiler_params=_cp(("arbitrary", "arbitrary")),
        name="nsa_prompt",
    )(z3, z3, kvcb, z3, z3, z3, z3, ovt)


def _nsa_sample_kernel(pt_ref, q_ref, gn_ref, kvs_ref, kvw_ref, win_ref, cmp_hbm, sel_hbm,
                       w1_ref, pe_ref, w2_ref, ov_ref, e_ref, o_ref,
                       cmp_buf, sel_buf, rows_scr, new_scr, sem, *, n_pages, past, t_new):
    b = pl.program_id(0)
    nb = pl.num_programs(0)
    slot = b % 2
    nch = past // CMP_STRIDE
    wb = win_ref.shape[-1]

    def page_copies(bb, sl, pg):
        toks = pl.ds(pg * PAGE, PAGE)
        p = pt_ref[bb, pg]
        return (pltpu.make_async_copy(cmp_hbm.at[p], cmp_buf.at[sl, :, :, toks], sem.at[0, sl]),
                pltpu.make_async_copy(sel_hbm.at[p], sel_buf.at[sl, :, :, toks], sem.at[1, sl]))

    def fetch(bb, sl):
        for pg in range(n_pages):
            for c in page_copies(bb, sl, pg):
                c.start()

    @pl.when(b == 0)
    def _():
        fetch(0, 0)

    @pl.when(b + 1 < nb)
    def _():
        fetch(b + 1, 1 - slot)

    for pg in range(n_pages):
        for c in page_copies(b, slot, pg):
            c.wait()

    new_scr[...] = jnp.zeros(new_scr.shape, F32)
    new_scr[0:SUB, 0:256] = kvs_ref[...]
    new_scr[0:SUB, 256:512] = kvw_ref[...]

    tblk = 512 if past % 512 == 0 else past

    def load(kv, tok):
        if tok == 0:
            for c0 in range(0, past, tblk):
                rows_scr[c0:c0 + tblk, :] = cmp_buf[slot, kv, :, c0:c0 + tblk].T
        return rows_scr[pl.ds(tok, nch, stride=CMP_STRIDE), :]

    kvcb = _compress(load, nch, w1_ref, pe_ref, w2_ref)

    tq = SUB
    n_slc = -(-(past + t_new) // SLC_BLOCK)
    q = q_ref[...]
    gn = gn_ref[...]
    npos = past + lax.broadcasted_iota(I32, (1, 128), 1)
    ppos = lax.broadcasted_iota(I32, (1, past), 1)
    wpos = past - wb + lax.broadcasted_iota(I32, (1, wb), 1)
    for g in range(NSA_KV):
        rows = slice(g * DH, (g + 1) * DH)
        q4 = _stack_heads(q, g)
        qpos, a_col, s_col = _head_cols(tq, past, g)
        o_cmp, sel = _cmp_branch(q4, kvcb[0][g], kvcb[1][g], ov_ref, qpos, a_col, s_col, tq, past, n_slc)
        sel4 = jnp.concatenate([sel[:, :128]] * NSA_GROUP, axis=0).astype(BF16)

        picked = _dot(sel4, e_ref[...]) > 0.5
        s_p = _bias_mask(_dot(q4, sel_buf[slot, 0, rows, :].astype(BF16)), ppos, picked & (ppos <= qpos), a_col, s_col)
        s_n = _masked_scores(q4, new_scr[:, rows].astype(BF16), npos, npos <= qpos, a_col, s_col)
        o_slc = _softmax_two(s_p, s_n, sel_buf[slot, 1, rows, :].astype(BF16),
                             new_scr[:, 128 + g * DH:128 + (g + 1) * DH].astype(BF16))

        s_p = _bias_mask(_dot(q4, win_ref[0, rows, :].astype(BF16)), wpos, (wpos <= qpos) & (wpos > qpos - WINDOW),
                         a_col, s_col)
        s_n = _masked_scores(q4, new_scr[:, 256 + g * DH:256 + (g + 1) * DH].astype(BF16), npos, npos <= qpos,
                             a_col, s_col)
        o_win = _softmax_two(s_p, s_n, win_ref[1, rows, :].astype(BF16),
                             new_scr[:, 384 + g * DH:384 + (g + 1) * DH].astype(BF16))

        o_ref[:, g * 256:(g + 1) * 256] = _gate_combine(gn, (o_cmp, o_slc, o_win), tq, g)


def _nsa_sample(z3, win, cache_cmp, cache_sel, page_table, w1t, pe_t, w2, ov, e_mat, t_new):
    bs = z3.shape[0]
    n_pages = page_table.shape[1]
    past = n_pages * PAGE
    wb = win.shape[-1]
    const = lambda shape: pl.BlockSpec(shape, lambda bi, pt, n=len(shape): (0,) * n)
    grid_spec = pltpu.PrefetchScalarGridSpec(
        num_scalar_prefetch=1,
        grid=(bs,),
        in_specs=[pl.BlockSpec((None, SUB, 512), lambda bi, pt: (bi, 0, C_Q // 512)),
                  pl.BlockSpec((None, SUB, 128), lambda bi, pt: (bi, 0, C_GN // 128)),
                  pl.BlockSpec((None, SUB, 256), lambda bi, pt: (bi, 0, C_KS // 256)),
                  pl.BlockSpec((None, SUB, 256), lambda bi, pt: (bi, 0, C_KW // 256)),
                  pl.BlockSpec((None, 2, 128, wb), lambda bi, pt: (bi, 0, 0, 0)),
                  pl.BlockSpec(memory_space=pl.ANY),
                  pl.BlockSpec(memory_space=pl.ANY),
                  const(w1t.shape), const(pe_t.shape), const(w2.shape), const(ov.shape), const(e_mat.shape)],
        out_specs=pl.BlockSpec((None, SUB, 512), lambda bi, pt: (bi, 0, 0)),
        scratch_shapes=[pltpu.VMEM((2, 2, 128, past), F32),
                        pltpu.VMEM((2, 2, 128, past), F32),
                        pltpu.VMEM((past, 128), F32),
                        pltpu.VMEM((128, 512), F32),
                        pltpu.SemaphoreType.DMA((2, 2))],
    )
    return pl.pallas_call(
        functools.partial(_nsa_sample_kernel, n_pages=n_pages, past=past, t_new=t_new),
        grid_spec=grid_spec,
        out_shape=jax.ShapeDtypeStruct((bs, SUB, 512), F32),
        compiler_params=_cp(("arbitrary",), 60 * 1024 * 1024),
        name="nsa_sample",
    )(page_table, z3, z3, z3, z3, win, cache_cmp, cache_sel, w1t, pe_t, w2, ov, e_mat)


def _hgrn_kernel(hq_ref, hf_ref, hi_ref, hg_ref, lb_ref, nw_ref, s0_ref, o_ref, sout_ref, st_scr,
                 *, tb, chunk, t_real, blk):
    j = pl.program_id(1)
    nw = nw_ref[...]
    row = lax.broadcasted_iota(I32, (chunk, 1), 0)
    real = row < t_real
    tri = (lax.broadcasted_iota(I32, (chunk, chunk), 0) >= lax.broadcasted_iota(I32, (chunk, chunk), 1)).astype(BF16)
    n_rows = min(chunk, t_real)

    @pl.when(j == 0)
    def _():
        for h in range(HG_HEADS):
            st_scr[h] = s0_ref[h].T

    def head_chunk(rows, h):
        cols = slice(h * HG_DK, (h + 1) * HG_DK)
        lb = lb_ref[h]
        q = _silu(hq_ref[rows, cols])
        f = lb + (1.0 - lb) * _sigmoid(hf_ref[rows, cols])
        k = jnp.where(real, 1.0 - f, 0.0)
        gl = jnp.where(real, jnp.log(f), 0.0)
        v = hi_ref[rows, cols]
        if chunk > 8:
            ga_, gb_, gc_ = _split3(gl)
            bcum = _dot(tri, ga_) + _dot(tri, gb_) + _dot(tri, gc_)
        else:
            bcum = jnp.zeros_like(gl)
            for s in range(t_real):
                bcum = bcum + jnp.where(row >= s, gl[s:s + 1, :], 0.0)
        st = st_scr[h]
        o_inter = _dot_nt((q * jnp.exp(bcum)).astype(BF16), st.astype(BF16))
        v16 = v.astype(BF16)
        parts = []
        for lo in range(0, chunk, blk):
            hi = min(lo + blk, chunk)
            acc = o_inter[lo:hi, :]
            if lo < n_rows:
                qj = q[lo:hi, :]
                bj = bcum[lo:hi, :]
                tj = row[lo:hi, :]
                for s in range(lo, min(hi, t_real)):
                    d = jnp.where(tj >= s, bj - bcum[s:s + 1, :], NEG)
                    w = jnp.sum(qj * jnp.exp(d) * k[s:s + 1, :], axis=-1, keepdims=True)
                    acc = acc + w * v[s:s + 1, :]
                if lo > 0:
                    ref = bcum[lo - 1:lo, :]
                    qt = (qj * jnp.exp(bj - ref)).astype(BF16)
                    kt = (k[0:lo, :] * jnp.exp(ref - bcum[0:lo, :])).astype(BF16)
                    acc = acc + _dot(_dot_nt(qt, kt).astype(BF16), v16[0:lo, :])
            parts.append(acc)
        o = parts[0] if len(parts) == 1 else jnp.concatenate(parts, axis=0)
        bc = bcum[chunk - 1:chunk, :]
        kt = k * jnp.exp(bc - bcum)
        st_scr[h] = st * jnp.exp(bc) + _dot_tn(v16, kt.astype(BF16))
        o = o * lax.rsqrt(jnp.mean(o * o, axis=-1, keepdims=True) + EPS) * nw
        o_ref[rows, cols] = o * _silu(hg_ref[rows, cols])

    def step(ci, carry):
        rows = pl.ds(pl.multiple_of(ci * chunk, chunk), chunk)
        for h in range(HG_HEADS):
            head_chunk(rows, h)
        return carry

    lax.fori_loop(0, tb // chunk, step, 0, unroll=4 if (tb // chunk) % 4 == 0 else 1)

    @pl.when(j == pl.num_programs(1) - 1)
    def _():
        for h in range(HG_HEADS):
            sout_ref[h] = st_scr[h].T


def _hgrn(z3, lb, norm_w, s0, chunk, t_real, tb, blk):
    b, t, _ = z3.shape
    w = HG_HEADS * HG_DK
    col = lambda c0: pl.BlockSpec((None, tb, w), lambda bi, j, c=c0 // w: (bi, j, c))
    state = pl.BlockSpec((None, HG_HEADS, HG_DK, HG_DV), lambda bi, j: (bi, 0, 0, 0))
    return pl.pallas_call(
        functools.partial(_hgrn_kernel, tb=tb, chunk=chunk, t_real=t_real, blk=blk),
        grid=(b, t // tb),
        in_specs=[col(C_HQ), col(C_HF), col(C_HI), col(C_HG),
                  pl.BlockSpec((HG_HEADS, 1, HG_DK), lambda bi, j: (0, 0, 0)),
                  pl.BlockSpec((1, HG_DV), lambda bi, j: (0, 0)),
                  state],
        out_specs=[pl.BlockSpec((None, tb, w), lambda bi, j: (bi, j, 0)), state],
        out_shape=[jax.ShapeDtypeStruct((b, t, w), F32),
                   jax.ShapeDtypeStruct((b, HG_HEADS, HG_DK, HG_DV), F32)],
        scratch_shapes=[pltpu.VMEM((HG_HEADS, HG_DV, HG_DK), F32)],
        compiler_params=_cp(("arbitrary", "arbitrary")),
        name="hgrn2",
    )(z3, z3, z3, z3, lb.reshape(HG_HEADS, 1, HG_DK), norm_w.reshape(1, HG_DV), s0)


def _merge_kernel(x_ref, on_ref, oh_ref, ga_ref, gb_ref, wa_ref, wb_ref, wo_ref, g1_ref, sc_ref, sh_ref,
                  npost_ref, npre_ref, wr_ref, br_ref, y_ref, h_ref, ti_ref, tw_ref):
    a = _dot(on_ref[...].astype(BF16), wa_ref[...])
    bb = _dot(oh_ref[...].astype(BF16), wb_ref[...])
    m = _sigmoid(ga_ref[...]) * a + _sigmoid(gb_ref[...]) * bb
    out = _dot(m.astype(BF16), wo_ref[...])
    y = x_ref[...] + g1_ref[...] * _rms(out, npost_ref[...])
    y_ref[...] = y
    h = _rms(y, npre_ref[...]) * (1.0 + sc_ref[...]) + sh_ref[...]
    h_ref[...] = h
    hh, hl, _ = _split3(h)
    w = wr_ref[...]
    wh = w.astype(BF16)
    wl = (w - wh.astype(F32)).astype(BF16)
    logits = _dot(hh, wh) + _dot(hh, wl) + _dot(hl, wh) + br_ref[...]
    lane = lax.broadcasted_iota(I32, logits.shape, 1)
    lanef = lane.astype(F32)
    logits = jnp.where(lane < N_EXPERTS, logits, -jnp.inf)
    ti = jnp.zeros(logits.shape, F32)
    tv = jnp.zeros(logits.shape, F32)
    v0 = None
    for kk in range(TOP_K):
        mx = jnp.max(logits, axis=-1, keepdims=True)
        idx = jnp.min(jnp.where(logits == mx, lanef, 1e9), axis=-1, keepdims=True)
        v0 = mx if v0 is None else v0
        ti = jnp.where(lane == kk, idx, ti)
        tv = jnp.where(lane == kk, jnp.exp(mx - v0), tv)
        logits = jnp.where(lanef == idx, -jnp.inf, logits)
    ti_ref[...] = ti.astype(I32)
    tw_ref[...] = tv / jnp.sum(tv, axis=-1, keepdims=True)


def _merge(x2d, o_nsa, o_hg, z2d, g1, sc2, sh2, wa, wb, wo, n_post, n_pre, w_router, b_router, rows_per_mod, tm):
    n, d = x2d.shape
    row = lambda w: pl.BlockSpec((tm, w), lambda i: (i, 0))
    if rows_per_mod:
        per = rows_per_mod // tm
        mod_spec = pl.BlockSpec((None, 1, d), lambda i: (i // per, 0, 0))
    else:
        mod_spec = row(d)
    const2 = lambda a: pl.BlockSpec(a.shape, lambda i: (0, 0))
    wr = jnp.pad(w_router, ((0, 0), (0, 128 - N_EXPERTS)))
    br = jnp.pad(b_router, (0, 128 - N_EXPERTS)).reshape(1, 128)
    vec = lambda v: v.reshape(1, d)
    return pl.pallas_call(
        _merge_kernel,
        grid=(n // tm,),
        in_specs=[row(d), row(512), row(512),
                  pl.BlockSpec((tm, d), lambda i: (i, C_GA // 1024)),
                  pl.BlockSpec((tm, d), lambda i: (i, C_GB // 1024)),
                  const2(wa), const2(wb), const2(wo),
                  mod_spec, mod_spec, mod_spec,
                  pl.BlockSpec((1, d), lambda i: (0, 0)), pl.BlockSpec((1, d), lambda i: (0, 0)),
                  const2(wr), const2(br)],
        out_specs=[row(d), row(d), row(128), row(128)],
        out_shape=[jax.ShapeDtypeStruct((n, d), F32), jax.ShapeDtypeStruct((n, d), F32),
                   jax.ShapeDtypeStruct((n, 128), I32), jax.ShapeDtypeStruct((n, 128), F32)],
        compiler_params=_cp(("arbitrary",)),
        name="merge_router",
    )(x2d, o_nsa, o_hg, z2d, z2d, wa, wb, wo, g1, sc2, sh2, vec(n_post), vec(n_pre), wr, br)


def _moe_kernel(wt_ref, we_ref, lo_ref, hi_ref, ni_ref, tok_ref, tok_next_ref, rw_ref, h_hbm,
                wu_ref, bu_ref, wd_ref, bd_ref, o_ref, xbuf, wu16, wd16, sem, *, tm, n_tiles):
    w = pl.program_id(0)
    tile = wt_ref[w]
    prev = jnp.maximum(w - 1, 0)
    first = (w == 0) | (tile != wt_ref[prev])
    valid = w < ni_ref[0]
    slot = tile % 2

    @pl.when(valid & ((w == 0) | (we_ref[w] != we_ref[prev])))
    def _():
        wu16[...] = wu_ref[...].astype(BF16)
        wd16[...] = wd_ref[...].astype(BF16)

    has_next = tile + 1 < n_tiles

    def row_copy(idx_ref, sl, r):
        return pltpu.make_async_copy(h_hbm.at[pl.ds(idx_ref[r], 1), :], xbuf.at[sl, pl.ds(r, 1), :], sem.at[sl])

    @pl.when(valid & (w == 0))
    def _():
        def body(r, c):
            row_copy(tok_ref, 0, r).start()
            return c
        lax.fori_loop(0, tm, body, 0, unroll=8)

    def ffn(issue_next, init):
        if init:
            pltpu.make_async_copy(h_hbm.at[pl.ds(0, tm), :], xbuf.at[slot], sem.at[slot]).wait()
        x = xbuf[slot].astype(BF16)
        if issue_next:
            for r in range(tm):
                row_copy(tok_next_ref, 1 - slot, r).start(priority=r % 2)
        z = _dot(x, wu16[...]) + bu_ref[...]
        de = wd_ref.shape[0]
        gate = jnp.minimum(z[:, :de], SWIGLU_LIMIT)
        up = jnp.clip(z[:, de:], -SWIGLU_LIMIT, SWIGLU_LIMIT)
        act = (up + 1.0) * gate * _sigmoid(SWIGLU_ALPHA * gate)
        y = _dot(act.astype(BF16), wd16[...]) + bd_ref[...]
        row = lax.broadcasted_iota(I32, (tm, 1), 0)
        mine = (row >= lo_ref[w]) & (row < hi_ref[w])
        contrib = jnp.where(mine, y * rw_ref[...], 0.0)
        o_ref[...] = contrib if init else o_ref[...] + contrib

    @pl.when(valid & first & has_next)
    def _():
        ffn(True, True)

    @pl.when(valid & first & jnp.logical_not(has_next))
    def _():
        ffn(False, True)

    @pl.when(valid & jnp.logical_not(first))
    def _():
        ffn(False, False)


def _moe_ffn(h2, items, row_token, row_weight, w_up, b_up, w_down, b_down, tm):
    n, d = h2.shape
    n_rows = row_token.shape[0]
    n_tiles = n_rows // tm
    ne, _, dh2 = w_up.shape
    wt, we, lo, hi, ni = items
    grid_spec = pltpu.PrefetchScalarGridSpec(
        num_scalar_prefetch=5,
        grid=(wt.shape[0],),
        in_specs=[pl.BlockSpec((tm,), lambda w, wt, *_: (wt[w],), memory_space=pltpu.SMEM),
                  pl.BlockSpec((tm,), lambda w, wt, *_: (jnp.minimum(wt[w] + 1, n_tiles - 1),), memory_space=pltpu.SMEM),
                  pl.BlockSpec((tm, 1), lambda w, wt, *_: (wt[w], 0)),
                  pl.BlockSpec(memory_space=pl.ANY),
                  pl.BlockSpec((None, d, dh2), lambda w, wt, we, *_: (we[w], 0, 0)),
                  pl.BlockSpec((None, 1, dh2), lambda w, wt, we, *_: (we[w], 0, 0)),
                  pl.BlockSpec((None, dh2 // 2, d), lambda w, wt, we, *_: (we[w], 0, 0)),
                  pl.BlockSpec((None, 1, d), lambda w, wt, we, *_: (we[w], 0, 0))],
        out_specs=pl.BlockSpec((tm, d), lambda w, wt, *_: (wt[w], 0)),
        scratch_shapes=[pltpu.VMEM((2, tm, d), F32), pltpu.VMEM((d, dh2), BF16), pltpu.VMEM((dh2 // 2, d), BF16),
                        pltpu.SemaphoreType.DMA((2,))],
    )
    return pl.pallas_call(
        functools.partial(_moe_kernel, tm=tm, n_tiles=n_tiles),
        grid_spec=grid_spec,
        out_shape=jax.ShapeDtypeStruct((n_rows, d), F32),
        compiler_params=_cp(("arbitrary",), 60 * 1024 * 1024),
        name="moe_ffn",
    )(wt, we, lo, hi, ni, row_token, row_token, row_weight, h2, w_up, b_up.reshape(ne, 1, dh2),
      w_down, b_down.reshape(ne, 1, d))


def _combine_kernel(pos_ref, y_hbm, y1_ref, g2_ref, npost_ref, o_ref, buf, sem, *, tm):
    def body(r, c):
        for kk in range(TOP_K):
            pltpu.make_async_copy(y_hbm.at[pl.ds(pos_ref[r * TOP_K + kk], 1), :], buf.at[kk, pl.ds(r, 1), :],
                                  sem.at[0]).start(priority=kk % 2)
        return c
    lax.fori_loop(0, tm, body, 0, unroll=4)
    for kk in range(TOP_K):
        pltpu.make_async_copy(y_hbm.at[pl.ds(0, tm), :], buf.at[kk], sem.at[0]).wait()
    moe = (buf[0] + buf[1]) + (buf[2] + buf[3])
    o_ref[...] = y1_ref[...] + g2_ref[...] * _rms(moe, npost_ref[...])


def _combine(pos_flat, y_sorted, y1, g2, n_post, rows_per_mod, tm):
    n, d = y1.shape
    if rows_per_mod:
        per = rows_per_mod // tm
        mod_spec = pl.BlockSpec((None, 1, d), lambda i: (i // per, 0, 0))
    else:
        mod_spec = pl.BlockSpec((tm, d), lambda i: (i, 0))
    return pl.pallas_call(
        functools.partial(_combine_kernel, tm=tm),
        grid=(n // tm,),
        in_specs=[pl.BlockSpec((tm * TOP_K,), lambda i: (i,), memory_space=pltpu.SMEM),
                  pl.BlockSpec(memory_space=pl.ANY),
                  pl.BlockSpec((tm, d), lambda i: (i, 0)),
                  mod_spec,
                  pl.BlockSpec((1, d), lambda i: (0, 0))],
        out_specs=pl.BlockSpec((tm, d), lambda i: (i, 0)),
        out_shape=jax.ShapeDtypeStruct((n, d), F32),
        scratch_shapes=[pltpu.VMEM((TOP_K, tm, d), F32), pltpu.SemaphoreType.DMA((1,))],
        compiler_params=_cp(("arbitrary",)),
        name="moe_combine",
    )(pos_flat, y_sorted, y1, g2, n_post.reshape(1, d))


def _route_tables(top_i, top_w, tm):
    m = top_i.shape[0] * TOP_K
    m_pad = -(-m // tm) * tm
    n_tiles = m_pad // tm
    e_flat = top_i.reshape(-1).astype(I32)
    iota = jnp.arange(m, dtype=I32)
    _, order, w_sorted = lax.sort((e_flat, iota, top_w.reshape(-1)), num_keys=1, is_stable=True)
    _, pos_of = lax.sort((order, iota), num_keys=1)
    row_token = jnp.pad(order // TOP_K, (0, m_pad - m))
    row_weight = jnp.pad(w_sorted, (0, m_pad - m)).reshape(m_pad, 1)
    ex = jnp.arange(N_EXPERTS, dtype=I32)
    counts = jnp.sum((e_flat[:, None] == ex[None, :]).astype(I32), axis=0)
    uend = jnp.cumsum(counts)
    ustart = uend - counts
    first_tile = ustart // tm
    n_item_e = jnp.where(counts > 0, (uend - 1) // tm - first_tile + 1, 0)
    iend = jnp.cumsum(n_item_e)
    n_items = iend[-1]
    wid = jnp.arange(n_tiles + N_EXPERTS - 1, dtype=I32)
    we = jnp.minimum(jnp.sum((wid[:, None] >= iend[None, :]).astype(I32), axis=1), N_EXPERTS - 1)
    onehot = (we[:, None] == ex[None, :]).astype(I32)
    pick = lambda v: jnp.sum(onehot * v[None, :], axis=1)
    wt = pick(first_tile) + (wid - pick(iend - n_item_e))
    lo = jnp.maximum(pick(ustart), wt * tm) - wt * tm
    hi = jnp.minimum(pick(uend), (wt + 1) * tm) - wt * tm
    live = wid < n_items
    last_e = jnp.max(jnp.where(counts > 0, ex, 0))
    items = (jnp.where(live, wt, n_tiles - 1), jnp.where(live, we, last_e),
             jnp.where(live, lo, 0), jnp.where(live, hi, 0), n_items.reshape(1))
    return tuple(a.astype(I32) for a in items), row_token, row_weight, pos_of


def _overlap_table(n_rows, n_cmp, n_cols):
    start = np.arange(n_rows)[:, None] * CMP_STRIDE
    j0 = np.arange(n_cols)[None, :] * SLC_BLOCK
    ov = (start < j0 + SLC_BLOCK) & (start + CMP_BLOCK > j0) & (np.arange(n_rows)[:, None] < n_cmp)
    return jnp.asarray(ov.astype(np.float32), dtype=BF16)


def _expand_table(n_rows, n_keys):
    e = (np.arange(n_keys)[None, :] // SLC_BLOCK) == np.arange(n_rows)[:, None]
    return jnp.asarray(e.astype(np.float32), dtype=BF16)


def _pack_w_in(w_in):
    d = w_in.shape[0]
    q = w_in[:, 0:512]
    kv = w_in[:, 512:1280]
    gn = w_in[:, 1280:1304]
    hh = w_in[:, 1304:3352]
    ga = w_in[:, 3352:4376]
    gb = w_in[:, 4376:5400]
    z = lambda w: jnp.zeros((d, w), w_in.dtype)
    return jnp.concatenate([ga, gb, q, hh, kv, gn, z(ZW - C_GN - 24)], axis=1).astype(BF16)


def _pack_compress(cmp_pe, cmp_w1, cmp_w2):
    r = CMP_BLOCK // CMP_STRIDE
    wt = cmp_w1.reshape(2, r, CMP_STRIDE, DH, CMP_HIDDEN).transpose(0, 2, 3, 1, 4).reshape(2, CMP_STRIDE, DH, r * CMP_HIDDEN)
    zero = jnp.zeros_like(wt)
    w1bd = jnp.concatenate([jnp.concatenate([wt, zero], axis=-1), jnp.concatenate([zero, wt], axis=-1)], axis=2)
    w1bd = w1bd.reshape(2, CMP_STRIDE * 2 * DH, 2 * r * CMP_HIDDEN)
    pe_t = jnp.pad(cmp_pe.reshape(2, r, CMP_STRIDE, DH), ((0, 0), (0, SUB - r), (0, 0), (0, DH)))
    return w1bd.astype(BF16), pe_t.reshape(2, SUB, CMP_STRIDE * 2 * DH), cmp_w2


def _kv_out(z3, c0, rows):
    b = z3.shape[0]
    return z3[:, rows, c0:c0 + 256].reshape(b, -1, 2, NSA_KV, DH)[None]


def kernel(x_prompt, x_sample, c_prompt, c_sample, cache_cmp, cache_sel, state_win, state_hgrn, page_table, w_ada, b_ada, norm_mix_pre, norm_mix_post, norm_ffn_pre, norm_ffn_post, w_in, cmp_pe, cmp_w1, cmp_w2, hg_lb_logits, hg_norm, w_branch_a, w_branch_b, w_out, w_router, b_router, w_up, b_up, w_down, b_down):
    bp, t, d = x_prompt.shape
    bs, ts, _ = x_sample.shape
    depth = w_in.shape[0]
    assert depth == 1 and ts <= SUB
    n_pool = cache_cmp.shape[1]
    past = page_table.shape[1] * PAGE
    lb_all = jnp.cumsum(jax.nn.softmax(hg_lb_logits.astype(F32), axis=0), axis=0)

    l = 0
    w_packed = _pack_w_in(w_in[l])
    w1t, pe_t, w2c = _pack_compress(cmp_pe[l], cmp_w1[l], cmp_w2[l])
    wa, wb, wo = w_branch_a[l].astype(BF16), w_branch_b[l].astype(BF16), w_out[l].astype(BF16)
    wu, wd = w_up[l], w_down[l]

    mod = _modulation(jnp.concatenate([c_prompt, c_sample], axis=0), w_ada[l], b_ada[l])
    mods = [m[:, None, :] for m in jnp.split(mod, 6, axis=-1)]
    sh1, sc1, g1, sh2, sc2, g2 = mods
    p_, s_ = slice(0, bp), slice(bp, bp + bs)

    tm_p = 256 if t % 256 == 0 else t
    zp = _inproj(x_prompt.reshape(bp * t, d), norm_mix_pre[l], sc1[p_], sh1[p_], w_packed, t, tm_p)
    zp3 = zp.reshape(bp, t, ZW)
    nch = t // CMP_STRIDE
    n_slc = -(-t // SLC_BLOCK)
    kvcb_p = _compress_prompt(zp3, w1t, pe_t, w2c)
    tq = 256 if t % 256 == 0 else t
    o_nsa_p = _nsa_prompt(zp3, kvcb_p, _overlap_table(nch, nch - 1, -(-n_slc // SUB) * SUB).T, tq)
    s0_p = jnp.zeros((bp, HG_HEADS, HG_DK, HG_DV), F32)
    chunk = int(np.gcd(t, HG_CHUNK))
    tb_h = 512 if t % 512 == 0 else t
    o_hg_p, hg_state_p = _hgrn(zp3, lb_all[l], hg_norm[l], s0_p, chunk, chunk, tb_h, 16 if chunk % 16 == 0 else chunk)
    tm_m = 512 if t % 512 == 0 else t
    y1_p, h2_p, ti_p, tw_p = _merge(x_prompt.reshape(bp * t, d), o_nsa_p.reshape(bp * t, 512), o_hg_p.reshape(bp * t, 512),
                                    zp, g1[p_], sc2[p_], sh2[p_], wa, wb, wo, norm_mix_post[l], norm_ffn_pre[l],
                                    w_router[l], b_router[l], t, tm_m)

    xs = jnp.pad(x_sample, ((0, 0), (0, SUB - ts), (0, 0))).reshape(bs * SUB, d)
    rep = lambda m: jnp.broadcast_to(m[s_], (bs, SUB, d)).reshape(bs * SUB, d)
    tm_s = 512 if (bs * SUB) % 512 == 0 else bs * SUB
    zs = _inproj(xs, norm_mix_pre[l], rep(sc1), rep(sh1), w_packed, 0, min(tm_s, 256))
    zs3 = zs.reshape(bs, SUB, ZW)
    tok_minor = lambda a: jnp.transpose(a, (0, 2, 3, 4, 1)).reshape(a.shape[0], 2, NSA_KV * DH, a.shape[1])
    o_nsa_s = _nsa_sample(zs3, tok_minor(state_win[l]), tok_minor(cache_cmp[l]), tok_minor(cache_sel[l]),
                          page_table, w1t, pe_t, w2c,
                          _overlap_table(past // CMP_STRIDE, past // CMP_STRIDE - 1, 256),
                          _expand_table(128, past), ts)
    o_hg_s, hg_state_s = _hgrn(zs3, lb_all[l], hg_norm[l], state_hgrn[l], SUB, ts, SUB, SUB)
    y1_s, h2_s, ti_s, tw_s = _merge(xs, o_nsa_s.reshape(bs * SUB, 512), o_hg_s.reshape(bs * SUB, 512), zs,
                                    rep(g1), rep(sc2), rep(sh2), wa, wb, wo, norm_mix_post[l], norm_ffn_pre[l],
                                    w_router[l], b_router[l], 0, tm_s)
    real = lambda a: a.reshape(bs, SUB, -1)[:, :ts].reshape(bs * ts, -1)

    h2 = jnp.concatenate([h2_p, real(h2_s)], axis=0)
    top_i = jnp.concatenate([ti_p[:, :TOP_K], real(ti_s)[:, :TOP_K]], axis=0)
    top_w = jnp.concatenate([tw_p[:, :TOP_K], real(tw_s)[:, :TOP_K]], axis=0)
    tm_e = 512
    items, row_token, row_weight, pos_of = _route_tables(top_i, top_w, tm_e)
    y_sorted = _moe_ffn(h2, items, row_token, row_weight, wu, b_up[l], wd, b_down[l], tm_e)
    np_tok = bp * t
    tm_cp = 256 if t % 256 == 0 else t
    y_p = _combine(pos_of[:np_tok * TOP_K], y_sorted, y1_p, g2[p_], norm_ffn_post[l], t, tm_cp).reshape(bp, t, d)
    g2_s = jnp.broadcast_to(g2[s_], (bs, ts, d)).reshape(bs * ts, d)
    tm_cs = 256 if (bs * ts) % 256 == 0 else bs * ts
    y_s = _combine(pos_of[np_tok * TOP_K:], y_sorted, real(y1_s), g2_s, norm_ffn_post[l], 0, tm_cs).reshape(bs, ts, d)
    keep = min(WINDOW, t)
    new_rows = zs3[:, :ts, C_KW:C_KW + 256].reshape(bs, ts, 2, NSA_KV, DH)
    win_s = jnp.concatenate([state_win[l], new_rows], axis=1)[:, ts:][None]
    return (y_p, y_s,
            _kv_out(zp3, C_KC, slice(0, t)), _kv_out(zs3, C_KC, slice(0, ts)),
            _kv_out(zp3, C_KS, slice(0, t)), _kv_out(zs3, C_KS, slice(0, ts)),
            _kv_out(zp3, C_KW, slice(t - keep, t)), win_s,
            hg_state_p[None], hg_state_s[None])
```

```python
import functools

import numpy as np
import jax
import jax.numpy as jnp
from jax import lax
from jax.experimental import pallas as pl
from jax.experimental.pallas import tpu as pltpu

F32, BF16, I32 = jnp.float32, jnp.bfloat16, jnp.int32

NSA_HEADS, NSA_KV, NSA_GROUP, DH = 8, 2, 4, 64
CMP_BLOCK, CMP_STRIDE, CMP_HIDDEN = 32, 16, 128
SLC_BLOCK, SLC_TOPK, WINDOW = 64, 16, 512
HG_HEADS, HG_DK, HG_DV, HG_CHUNK = 4, 128, 128, 64
N_EXPERTS, TOP_K = 32, 4
SWIGLU_LIMIT, SWIGLU_ALPHA = 7.0, 1.702
EPS, NEG, FORCE = 1e-6, -1e30, 1e4
PAGE = 128

C_GA, C_GB, C_Q, C_HQ, C_HF, C_HI, C_HG = 0, 1024, 2048, 2560, 3072, 3584, 4096
C_KC, C_KS, C_KW, C_GN, ZW = 4608, 4864, 5120, 5376, 5632
SUB = 8

VMEM_LIMIT = 56 * 1024 * 1024


def _cp(sem, vmem=VMEM_LIMIT):
    return pltpu.CompilerParams(dimension_semantics=sem, vmem_limit_bytes=vmem)


def _dot(a, b):
    return jnp.dot(a, b, preferred_element_type=F32)


def _dot_nt(a, b):
    return lax.dot_general(a, b, (((1,), (1,)), ((), ())), preferred_element_type=F32)


def _dot_tn(a, b):
    return lax.dot_general(a, b, (((0,), (0,)), ((), ())), preferred_element_type=F32)


def _split3(x):
    a = x.astype(BF16)
    r = x - a.astype(F32)
    b = r.astype(BF16)
    c = (r - b.astype(F32)).astype(BF16)
    return a, b, c


def _dot3(x, w_bf16):
    a, b, c = _split3(x)
    return _dot(a, w_bf16) + _dot(b, w_bf16) + _dot(c, w_bf16)


def _sigmoid(x):
    return 1.0 / (1.0 + jnp.exp(-x))


def _silu(x):
    return x * _sigmoid(x)


def _rms(x, g):
    return x * lax.rsqrt(jnp.mean(x * x, axis=-1, keepdims=True) + EPS) * g


def _gelu_tanh(x):
    return 0.5 * x * (1.0 + jnp.tanh(0.7978845608028654 * (x + 0.044715 * (x * x * x))))


def _mod_kernel(c_ref, w_ref, b_ref, o_ref):
    o_ref[...] = _dot(_silu(c_ref[...]).astype(BF16), w_ref[...].astype(BF16)) + b_ref[...]


def _modulation(c, w_ada, b_ada):
    m, d = c.shape
    nw = w_ada.shape[1]
    tn = nw // 6
    return pl.pallas_call(
        _mod_kernel,
        grid=(nw // tn,),
        in_specs=[pl.BlockSpec((m, d), lambda j: (0, 0)),
                  pl.BlockSpec((d, tn), lambda j: (0, j)),
                  pl.BlockSpec((1, tn), lambda j: (0, j))],
        out_specs=pl.BlockSpec((m, tn), lambda j: (0, j)),
        out_shape=jax.ShapeDtypeStruct((m, nw), F32),
        compiler_params=_cp(("arbitrary",)),
        name="modulation",
    )(c, w_ada, b_ada.reshape(1, nw))


INPROJ_TN = 512


def _inproj_kernel(x_ref, g_ref, sc_ref, sh_ref, w_ref, o_ref):
    h = (_rms(x_ref[...], g_ref[...]) * (1.0 + sc_ref[...]) + sh_ref[...]).astype(BF16)
    for c0 in range(0, o_ref.shape[1], INPROJ_TN):
        o_ref[:, c0:c0 + INPROJ_TN] = _dot(h, w_ref[:, c0:c0 + INPROJ_TN])


def _inproj(x2d, g, sc, sh, w_packed, rows_per_mod, tm):
    n, d = x2d.shape
    zw = w_packed.shape[1]
    if rows_per_mod:
        per = rows_per_mod // tm
        mod_spec = pl.BlockSpec((None, 1, d), lambda i: (i // per, 0, 0))
    else:
        mod_spec = pl.BlockSpec((tm, d), lambda i: (i, 0))
    return pl.pallas_call(
        _inproj_kernel,
        grid=(n // tm,),
        in_specs=[pl.BlockSpec((tm, d), lambda i: (i, 0)),
                  pl.BlockSpec((1, d), lambda i: (0, 0)),
                  mod_spec, mod_spec,
                  pl.BlockSpec((d, zw), lambda i: (0, 0))],
        out_specs=pl.BlockSpec((tm, zw), lambda i: (i, 0)),
        out_shape=jax.ShapeDtypeStruct((n, zw), F32),
        compiler_params=_cp(("arbitrary",)),
        name="inproj",
    )(x2d, g.reshape(1, d), sc, sh, w_packed)


def _compress(load, nch, w1_ref, pe_ref, w2_ref):
    out = []
    for kv in range(2):
        w1 = w1_ref[kv]
        x = jnp.concatenate([load(kv, tok) for tok in range(CMP_STRIDE)], axis=1).astype(BF16)
        acc = _dot(x, w1)
        pe = _dot(pe_ref[kv].astype(BF16), w1)
        pre0 = pe[0:1, :CMP_HIDDEN] + pe[1:2, CMP_HIDDEN:2 * CMP_HIDDEN]
        w2 = w2_ref[kv].astype(BF16)
        row = []
        for g in range(2):
            c0 = g * 2 * CMP_HIDDEN
            pre = (acc[:, c0:c0 + CMP_HIDDEN]
                   + pltpu.roll(acc[:, c0 + CMP_HIDDEN:c0 + 2 * CMP_HIDDEN], nch - 1, 0) + pre0)
            row.append(_dot(_gelu_tanh(pre).astype(BF16), w2))
        out.append(row)
    return out


def _compress_prompt_kernel(k_ref, v_ref, w1_ref, pe_ref, w2_ref, o_ref, *, nch):
    def load(kv, tok):
        return (k_ref, v_ref)[kv][pl.ds(tok, nch, stride=CMP_STRIDE), :]

    out = _compress(load, nch, w1_ref, pe_ref, w2_ref)
    for kv in range(2):
        for g in range(2):
            o_ref[kv, g] = out[kv][g]


def _compress_prompt(z3, w1t, pe_t, w2):
    b, t, _ = z3.shape
    nch = t // CMP_STRIDE
    return pl.pallas_call(
        functools.partial(_compress_prompt_kernel, nch=nch),
        grid=(b,),
        in_specs=[pl.BlockSpec((None, t, 128), lambda i: (i, 0, C_KC // 128)),
                  pl.BlockSpec((None, t, 128), lambda i: (i, 0, C_KC // 128 + 1)),
                  pl.BlockSpec(w1t.shape, lambda i: (0, 0, 0)),
                  pl.BlockSpec(pe_t.shape, lambda i: (0, 0, 0)),
                  pl.BlockSpec(w2.shape, lambda i: (0, 0, 0))],
        out_specs=pl.BlockSpec((None, 2, 2, nch, DH), lambda i: (i, 0, 0, 0, 0)),
        out_shape=jax.ShapeDtypeStruct((b, 2, 2, nch, DH), F32),
        compiler_params=_cp(("arbitrary",)),
        name="compress_prompt",
    )(z3, z3, w1t, pe_t, w2)


def _slope(g, r):
    return 2.0 ** (-(g * NSA_GROUP + r + 1))


def _stack_heads(q, g):
    parts = [q[:, (g * NSA_GROUP + r) * DH:(g * NSA_GROUP + r + 1) * DH] for r in range(NSA_GROUP)]
    return (jnp.concatenate(parts, axis=0) * (DH ** -0.5)).astype(BF16)


def _head_cols(tq, q0, g):
    pos1 = q0 + lax.broadcasted_iota(I32, (tq, 1), 0)
    qpos = jnp.concatenate([pos1] * NSA_GROUP, axis=0)
    slope = jnp.concatenate([jnp.full((tq, 1), _slope(g, r), F32) for r in range(NSA_GROUP)], axis=0)
    return qpos, -slope * qpos.astype(F32), slope


def _bias_mask(s, kpos_row, valid, a_col, s_col):
    return jnp.where(valid, s + (a_col + s_col * kpos_row.astype(F32)), NEG)


def _masked_scores(q4, k_bf16, kpos_row, valid, a_col, s_col):
    return _bias_mask(_dot_nt(q4, k_bf16), kpos_row, valid, a_col, s_col)


def _softmax_two(s_p, s_n, vt_p, v_n):
    m = jnp.maximum(jnp.max(s_p, axis=-1, keepdims=True), jnp.max(s_n, axis=-1, keepdims=True))
    p_p = jnp.exp(s_p - m)
    p_n = jnp.exp(s_n - m)
    l = jnp.sum(p_p, axis=-1, keepdims=True) + jnp.sum(p_n, axis=-1, keepdims=True)
    return (_dot_nt(p_p.astype(BF16), vt_p) + _dot(p_n.astype(BF16), v_n)) / l


def _topk_mask(score, k, n):
    lane = lax.broadcasted_iota(I32, score.shape, 1)
    ahead = jnp.zeros(score.shape, F32)
    for i in range(n):
        col = score[:, i:i + 1]
        ahead = ahead + jnp.where((col > score) | ((col == score) & (lane > i)), 1.0, 0.0)
    return jnp.where((ahead < k) & (lane < n), 1.0, 0.0)


def _cmp_attend(q4, kcb, vcb, qpos, a_col, s_col, tq):
    ncmp = kcb.shape[0]
    endpos = lax.broadcasted_iota(I32, (1, ncmp), 1) * CMP_STRIDE + (CMP_BLOCK - 1)
    valid = endpos <= qpos
    s = _masked_scores(q4, kcb.astype(BF16), endpos, valid, a_col, s_col)
    m = jnp.max(s, axis=-1, keepdims=True)
    e = jnp.exp(s - m)
    p = jnp.where(valid, e, 0.0) / jnp.sum(e, axis=-1, keepdims=True)
    o = _dot(p.astype(BF16), vcb.astype(BF16))
    return o, p[0:tq] + p[tq:2 * tq] + p[2 * tq:3 * tq] + p[3 * tq:4 * tq]


def _cmp_branch(q4, kcb, vcb, ov_ref, qpos, a_col, s_col, tq, q0, n_slc):
    o, psum = _cmp_attend(q4, kcb, vcb, qpos, a_col, s_col, tq)
    imp = _dot3(psum, ov_ref[...])
    jl = lax.broadcasted_iota(I32, imp.shape, 1)
    cur = lax.shift_right_logical(q0 + lax.broadcasted_iota(I32, (tq, 1), 0), 6)
    forced = (jl == 0) | (jl == cur) | (jl == cur - 1)
    score = jnp.where(forced, FORCE, jnp.where(jl <= cur, imp, -FORCE))
    return o, _topk_mask(score, min(SLC_TOPK, n_slc), n_slc)


def _select_cols(imp, tq, q0, n_slc):
    jr = lax.broadcasted_iota(I32, imp.shape, 0)
    cur = lax.shift_right_logical(q0 + lax.broadcasted_iota(I32, (1, tq), 1), 6)
    forced = (jr == 0) | (jr == cur) | (jr == cur - 1)
    score = jnp.where(forced, FORCE, jnp.where(jr <= cur, imp, -FORCE))
    ahead = jnp.zeros(score.shape, F32)
    for i in range(n_slc):
        rw = score[i:i + 1, :]
        ahead = ahead + jnp.where((rw > score) | ((rw == score) & (jr > i)), 1.0, 0.0)
    return jnp.where((ahead < min(SLC_TOPK, n_slc)) & (jr < n_slc), 1.0, 0.0)


def _online(s, carry, v16):
    m, l, acc = carry
    m_new = jnp.maximum(m, jnp.max(s, axis=-1, keepdims=True))
    alpha = jnp.exp(m - m_new)
    p = jnp.exp(s - m_new)
    return m_new, alpha * l + jnp.sum(p, axis=-1, keepdims=True), alpha * acc + _dot(p.astype(BF16), v16)


def _gate_combine(gn, outs, tq, g):
    sig = _sigmoid(gn)
    heads = []
    for r in range(NSA_GROUP):
        acc = None
        for br, o in enumerate(outs):
            c = br * NSA_HEADS + g * NSA_GROUP + r
            t = sig[:, c:c + 1] * o[r * tq:(r + 1) * tq, :]
            acc = t if acc is None else acc + t
        heads.append(acc)
    return jnp.concatenate(heads, axis=1)


MASK_BIG = 2.0 ** 60
N_SEL_COLS = 32
COL_POS_HI, COL_POS_LO = 32, 33


def _nsa_prompt_kernel(q_ref, gn_ref, kvc_ref, ks_ref, vs_ref, kw_ref, vw_ref, ovt_ref, o_ref,
                       ksa_scr, kwa_scr, vst_scr, vwt_scr, *, tq, t_len, n_slc):
    i = pl.program_id(1)
    q0 = i * tq
    nr = NSA_GROUP * tq

    @pl.when(i == 0)
    def _():
        kp = lax.broadcasted_iota(I32, (t_len, DH), 0)
        ln = lax.broadcasted_iota(I32, (t_len, DH), 1)
        pos_cols = jnp.where(ln == COL_POS_HI, lax.shift_right_logical(kp, 7).astype(F32),
                             jnp.where(ln == COL_POS_LO, (kp & 127).astype(F32), 0.0))
        sel_cols = jnp.where(ln == lax.shift_right_logical(kp, 6), 1.0, pos_cols)
        for g in range(NSA_KV):
            lanes = slice(g * DH, (g + 1) * DH)
            ksa_scr[g] = jnp.concatenate([ks_ref[:, lanes], sel_cols], axis=1).astype(BF16)
            kwa_scr[g] = jnp.concatenate([kw_ref[:, lanes], pos_cols], axis=1).astype(BF16)
        for c0 in range(0, t_len, tq):
            vst_scr[:, c0:c0 + tq] = vs_ref[c0:c0 + tq, :].T.astype(BF16)
            vwt_scr[:, c0:c0 + tq] = vw_ref[c0:c0 + tq, :].T.astype(BF16)

    lane = lax.broadcasted_iota(I32, (1, nr), 1)
    qpos = q0 + (lane & (tq - 1))
    head = lax.shift_right_logical(lane, tq.bit_length() - 1)
    krow = lax.broadcasted_iota(I32, (tq, 1), 0)
    sig_t = _sigmoid(gn_ref[...].T)
    init = (jnp.full((1, nr), NEG, F32), jnp.zeros((1, nr), F32), jnp.zeros((DH, nr), F32))
    qas, o_cmps = [], []
    for g in range(NSA_KV):
        slope = jnp.where(head == 0, _slope(g, 0), jnp.where(head == 1, _slope(g, 1),
                          jnp.where(head == 2, _slope(g, 2), _slope(g, 3))))
        qt = q_ref[:, g * 256:(g + 1) * 256].T * (DH ** -0.5)
        q4t = jnp.concatenate([qt[r * DH:(r + 1) * DH, :] for r in range(NSA_GROUP)], axis=1)

        ncmp = kvc_ref.shape[2]
        endpos = lax.broadcasted_iota(I32, (ncmp, 1), 0) * CMP_STRIDE + (CMP_BLOCK - 1)
        valid = endpos <= qpos
        s = _dot(kvc_ref[0, g].astype(BF16), q4t.astype(BF16)) + slope * endpos.astype(F32)
        s = jnp.where(valid, s, NEG)
        e = jnp.exp(s - jnp.max(s, axis=0, keepdims=True))
        p = jnp.where(valid, e, 0.0) / jnp.sum(e, axis=0, keepdims=True)
        o_cmp = _dot_tn(kvc_ref[1, g].astype(BF16), p.astype(BF16))
        psum_t = p[:, 0:tq] + p[:, tq:2 * tq] + p[:, 2 * tq:3 * tq] + p[:, 3 * tq:4 * tq]
        imp_t = sum(_dot(ovt_ref[...], part) for part in _split3(psum_t))
        sel_t = _select_cols(imp_t, tq, q0, n_slc)
        sel4 = jnp.concatenate([sel_t] * NSA_GROUP, axis=1)
        jr = lax.broadcasted_iota(I32, (N_SEL_COLS, nr), 0)
        pos_rows = jnp.where(jr == COL_POS_HI - N_SEL_COLS, 128.0 * slope,
                             jnp.where(jr == COL_POS_LO - N_SEL_COLS, slope, 0.0))
        pieces = [q4t, (sel4 - 1.0) * MASK_BIG]
        if sel4.shape[0] < N_SEL_COLS:
            pieces.append(jnp.zeros((N_SEL_COLS - sel4.shape[0], nr), F32))
        qas.append(jnp.concatenate(pieces + [pos_rows], axis=0).astype(BF16))
        o_cmps.append(o_cmp)

    def attend(ka_ref, vt_ref, kt_lo, band):
        def tile(kt, carries, diag):
            k0 = pl.multiple_of(kt * tq, tq)
            out = []
            for g in range(NSA_KV):
                s = _dot(ka_ref[g, pl.ds(k0, tq), :], qas[g])
                if band:
                    s = jnp.where(k0 + krow > qpos - WINDOW, s, NEG)
                if diag:
                    s = jnp.where(k0 + krow <= qpos, s, NEG)
                m, l, acc = carries[g]
                m_new = jnp.maximum(m, jnp.max(s, axis=0, keepdims=True))
                alpha = jnp.exp(m - m_new)
                pt = jnp.exp(s - m_new)
                l = alpha * l + jnp.sum(pt, axis=0, keepdims=True)
                acc = alpha * acc + _dot(vt_ref[g * DH:(g + 1) * DH, pl.ds(k0, tq)], pt.astype(BF16))
                out.append((m_new, l, acc))
            return tuple(out)

        carries = lax.fori_loop(kt_lo, i, lambda kt, c: tile(kt, c, False), (init,) * NSA_KV)
        return [acc / l for _, l, acc in tile(i, carries, True)]

    o_slcs = attend(ksa_scr, vst_scr, 0, False)
    o_wins = attend(kwa_scr, vwt_scr, jnp.maximum(i - WINDOW // tq, 0), True)

    for g in range(NSA_KV):
        o_cmp, o_slc, o_win = o_cmps[g], o_slcs[g], o_wins[g]
        heads = []
        for r in range(NSA_GROUP):
            cols = slice(r * tq, (r + 1) * tq)
            acc = None
            for br, o in enumerate((o_cmp, o_slc, o_win)):
                c = br * NSA_HEADS + g * NSA_GROUP + r
                t = sig_t[c:c + 1, :] * o[:, cols]
                acc = t if acc is None else acc + t
            heads.append(acc)
        o_ref[:, g * 256:(g + 1) * 256] = jnp.concatenate(heads, axis=0).T


def _nsa_prompt(z3, kvcb, ovt, tq):
    b, t, _ = z3.shape
    n_slc = -(-t // SLC_BLOCK)
    assert n_slc <= N_SEL_COLS and WINDOW % tq == 0 and t % tq == 0 and tq & (tq - 1) == 0
    kv_spec = lambda col: pl.BlockSpec((None, t, 128), lambda bi, i, c=col // 128: (bi, 0, c))
    return pl.pallas_call(
        functools.partial(_nsa_prompt_kernel, tq=tq, t_len=t, n_slc=n_slc),
        grid=(b, t // tq),
        in_specs=[pl.BlockSpec((None, tq, 512), lambda bi, i: (bi, i, C_Q // 512)),
                  pl.BlockSpec((None, tq, 128), lambda bi, i: (bi, i, C_GN // 128)),
                  pl.BlockSpec((None,) + kvcb.shape[1:], lambda bi, i: (bi, 0, 0, 0, 0)),
                  kv_spec(C_KS), kv_spec(C_KS + 128), kv_spec(C_KW), kv_spec(C_KW + 128),
                  pl.BlockSpec(ovt.shape, lambda bi, i: (0, 0))],
        out_specs=pl.BlockSpec((None, tq, 512), lambda bi, i: (bi, i, 0)),
        out_shape=jax.ShapeDtypeStruct((b, t, 512), F32),
        scratch_shapes=[pltpu.VMEM((NSA_KV, t, 2 * DH), BF16), pltpu.VMEM((NSA_KV, t, 2 * DH), BF16),
                        pltpu.VMEM((2 * DH, t), BF16), pltpu.VMEM((2 * DH, t), BF16)],
        compiler_params=_cp(("arbitrary", "arbitrary")),
        name="nsa_prompt",
    )(z3, z3, kvcb, z3, z3, z3, z3, ovt)


def _nsa_sample_kernel(pt_ref, q_ref, gn_ref, kvs_ref, kvw_ref, win_ref, cmp_hbm, sel_hbm,
                       w1_ref, pe_ref, w2_ref, ov_ref, e_ref, o_ref,
                       cmp_buf, sel_buf, rows_scr, new_scr, sem, *, n_pages, past, t_new):
    b = pl.program_id(0)
    nb = pl.num_programs(0)
    slot = b % 2
    nch = past // CMP_STRIDE
    wb = win_ref.shape[-1]

    def page_copies(bb, sl, pg):
        toks = pl.ds(pg * PAGE, PAGE)
        p = pt_ref[bb, pg]
        return (pltpu.make_async_copy(cmp_hbm.at[p], cmp_buf.at[sl, :, :, toks], sem.at[0, sl]),
                pltpu.make_async_copy(sel_hbm.at[p], sel_buf.at[sl, :, :, toks], sem.at[1, sl]))

    def fetch(bb, sl):
        for pg in range(n_pages):
            for c in page_copies(bb, sl, pg):
                c.start()

    @pl.when(b == 0)
    def _():
        fetch(0, 0)

    @pl.when(b + 1 < nb)
    def _():
        fetch(b + 1, 1 - slot)

    for pg in range(n_pages):
        for c in page_copies(b, slot, pg):
            c.wait()

    new_scr[...] = jnp.zeros(new_scr.shape, F32)
    new_scr[0:SUB, 0:256] = kvs_ref[...]
    new_scr[0:SUB, 256:512] = kvw_ref[...]

    tblk = 512 if past % 512 == 0 else past

    def load(kv, tok):
        if tok == 0:
            for c0 in range(0, past, tblk):
                rows_scr[c0:c0 + tblk, :] = cmp_buf[slot, kv, :, c0:c0 + tblk].T
        return rows_scr[pl.ds(tok, nch, stride=CMP_STRIDE), :]

    kvcb = _compress(load, nch, w1_ref, pe_ref, w2_ref)

    tq = SUB
    n_slc = -(-(past + t_new) // SLC_BLOCK)
    q = q_ref[...]
    gn = gn_ref[...]
    npos = past + lax.broadcasted_iota(I32, (1, 128), 1)
    ppos = lax.broadcasted_iota(I32, (1, past), 1)
    wpos = past - wb + lax.broadcasted_iota(I32, (1, wb), 1)
    for g in range(NSA_KV):
        rows = slice(g * DH, (g + 1) * DH)
        q4 = _stack_heads(q, g)
        qpos, a_col, s_col = _head_cols(tq, past, g)
        o_cmp, sel = _cmp_branch(q4, kvcb[0][g], kvcb[1][g], ov_ref, qpos, a_col, s_col, tq, past, n_slc)
        sel4 = jnp.concatenate([sel[:, :128]] * NSA_GROUP, axis=0).astype(BF16)

        picked = _dot(sel4, e_ref[...]) > 0.5
        s_p = _bias_mask(_dot(q4, sel_buf[slot, 0, rows, :].astype(BF16)), ppos, picked & (ppos <= qpos), a_col, s_col)
        s_n = _masked_scores(q4, new_scr[:, rows].astype(BF16), npos, npos <= qpos, a_col, s_col)
        o_slc = _softmax_two(s_p, s_n, sel_buf[slot, 1, rows, :].astype(BF16),
                             new_scr[:, 128 + g * DH:128 + (g + 1) * DH].astype(BF16))

        s_p = _bias_mask(_dot(q4, win_ref[0, rows, :].astype(BF16)), wpos, (wpos <= qpos) & (wpos > qpos - WINDOW),
                         a_col, s_col)
        s_n = _masked_scores(q4, new_scr[:, 256 + g * DH:256 + (g + 1) * DH].astype(BF16), npos, npos <= qpos,
                             a_col, s_col)
        o_win = _softmax_two(s_p, s_n, win_ref[1, rows, :].astype(BF16),
                             new_scr[:, 384 + g * DH:384 + (g + 1) * DH].astype(BF16))

        o_ref[:, g * 256:(g + 1) * 256] = _gate_combine(gn, (o_cmp, o_slc, o_win), tq, g)


def _nsa_sample(z3, win, cache_cmp, cache_sel, page_table, w1t, pe_t, w2, ov, e_mat, t_new):
    bs = z3.shape[0]
    n_pages = page_table.shape[1]
    past = n_pages * PAGE
    wb = win.shape[-1]
    const = lambda shape: pl.BlockSpec(shape, lambda bi, pt, n=len(shape): (0,) * n)
    grid_spec = pltpu.PrefetchScalarGridSpec(
        num_scalar_prefetch=1,
        grid=(bs,),
        in_specs=[pl.BlockSpec((None, SUB, 512), lambda bi, pt: (bi, 0, C_Q // 512)),
                  pl.BlockSpec((None, SUB, 128), lambda bi, pt: (bi, 0, C_GN // 128)),
                  pl.BlockSpec((None, SUB, 256), lambda bi, pt: (bi, 0, C_KS // 256)),
                  pl.BlockSpec((None, SUB, 256), lambda bi, pt: (bi, 0, C_KW // 256)),
                  pl.BlockSpec((None, 2, 128, wb), lambda bi, pt: (bi, 0, 0, 0)),
                  pl.BlockSpec(memory_space=pl.ANY),
                  pl.BlockSpec(memory_space=pl.ANY),
                  const(w1t.shape), const(pe_t.shape), const(w2.shape), const(ov.shape), const(e_mat.shape)],
        out_specs=pl.BlockSpec((None, SUB, 512), lambda bi, pt: (bi, 0, 0)),
        scratch_shapes=[pltpu.VMEM((2, 2, 128, past), F32),
                        pltpu.VMEM((2, 2, 128, past), F32),
                        pltpu.VMEM((past, 128), F32),
                        pltpu.VMEM((128, 512), F32),
                        pltpu.SemaphoreType.DMA((2, 2))],
    )
    return pl.pallas_call(
        functools.partial(_nsa_sample_kernel, n_pages=n_pages, past=past, t_new=t_new),
        grid_spec=grid_spec,
        out_shape=jax.ShapeDtypeStruct((bs, SUB, 512), F32),
        compiler_params=_cp(("arbitrary",), 60 * 1024 * 1024),
        name="nsa_sample",
    )(page_table, z3, z3, z3, z3, win, cache_cmp, cache_sel, w1t, pe_t, w2, ov, e_mat)


def _hgrn_kernel(hq_ref, hf_ref, hi_ref, hg_ref, lb_ref, nw_ref, s0_ref, o_ref, sout_ref, st_scr,
                 *, tb, chunk, t_real, blk):
    j = pl.program_id(1)
    nw = nw_ref[...]
    row = lax.broadcasted_iota(I32, (chunk, 1), 0)
    real = row < t_real
    tri = (lax.broadcasted_iota(I32, (chunk, chunk), 0) >= lax.broadcasted_iota(I32, (chunk, chunk), 1)).astype(BF16)
    n_rows = min(chunk, t_real)

    @pl.when(j == 0)
    def _():
        for h in range(HG_HEADS):
            st_scr[h] = s0_ref[h].T

    def head_chunk(rows, h):
        cols = slice(h * HG_DK, (h + 1) * HG_DK)
        lb = lb_ref[h]
        q = _silu(hq_ref[rows, cols])
        f = lb + (1.0 - lb) * _sigmoid(hf_ref[rows, cols])
        k = jnp.where(real, 1.0 - f, 0.0)
        gl = jnp.where(real, jnp.log(f), 0.0)
        v = hi_ref[rows, cols]
        if chunk > 8:
            ga_, gb_, gc_ = _split3(gl)
            bcum = _dot(tri, ga_) + _dot(tri, gb_) + _dot(tri, gc_)
        else:
            bcum = jnp.zeros_like(gl)
            for s in range(t_real):
                bcum = bcum + jnp.where(row >= s, gl[s:s + 1, :], 0.0)
        st = st_scr[h]
        o_inter = _dot_nt((q * jnp.exp(bcum)).astype(BF16), st.astype(BF16))
        v16 = v.astype(BF16)
        parts = []
        for lo in range(0, chunk, blk):
            hi = min(lo + blk, chunk)
            acc = o_inter[lo:hi, :]
            if lo < n_rows:
                qj = q[lo:hi, :]
                bj = bcum[lo:hi, :]
                tj = row[lo:hi, :]
                for s in range(lo, min(hi, t_real)):
                    d = jnp.where(tj >= s, bj - bcum[s:s + 1, :], NEG)
                    w = jnp.sum(qj * jnp.exp(d) * k[s:s + 1, :], axis=-1, keepdims=True)
                    acc = acc + w * v[s:s + 1, :]
                if lo > 0:
                    ref = bcum[lo - 1:lo, :]
                    qt = (qj * jnp.exp(bj - ref)).astype(BF16)
                    kt = (k[0:lo, :] * jnp.exp(ref - bcum[0:lo, :])).astype(BF16)
                    acc = acc + _dot(_dot_nt(qt, kt).astype(BF16), v16[0:lo, :])
            parts.append(acc)
        o = parts[0] if len(parts) == 1 else jnp.concatenate(parts, axis=0)
        bc = bcum[chunk - 1:chunk, :]
        kt = k * jnp.exp(bc - bcum)
        st_scr[h] = st * jnp.exp(bc) + _dot_tn(v16, kt.astype(BF16))
        o = o * lax.rsqrt(jnp.mean(o * o, axis=-1, keepdims=True) + EPS) * nw
        o_ref[rows, cols] = o * _silu(hg_ref[rows, cols])

    def step(ci, carry):
        rows = pl.ds(pl.multiple_of(ci * chunk, chunk), chunk)
        for h in range(HG_HEADS):
            head_chunk(rows, h)
        return carry

    lax.fori_loop(0, tb // chunk, step, 0, unroll=True)

    @pl.when(j == pl.num_programs(1) - 1)
    def _():
        for h in range(HG_HEADS):
            sout_ref[h] = st_scr[h].T


def _hgrn(z3, lb, norm_w, s0, chunk, t_real, tb, blk):
    b, t, _ = z3.shape
    w = HG_HEADS * HG_DK
    col = lambda c0: pl.BlockSpec((None, tb, w), lambda bi, j, c=c0 // w: (bi, j, c))
    state = pl.BlockSpec((None, HG_HEADS, HG_DK, HG_DV), lambda bi, j: (bi, 0, 0, 0))
    return pl.pallas_call(
        functools.partial(_hgrn_kernel, tb=tb, chunk=chunk, t_real=t_real, blk=blk),
        grid=(b, t // tb),
        in_specs=[col(C_HQ), col(C_HF), col(C_HI), col(C_HG),
                  pl.BlockSpec((HG_HEADS, 1, HG_DK), lambda bi, j: (0, 0, 0)),
                  pl.BlockSpec((1, HG_DV), lambda bi, j: (0, 0)),
                  state],
        out_specs=[pl.BlockSpec((None, tb, w), lambda bi, j: (bi, j, 0)), state],
        out_shape=[jax.ShapeDtypeStruct((b, t, w), F32),
                   jax.ShapeDtypeStruct((b, HG_HEADS, HG_DK, HG_DV), F32)],
        scratch_shapes=[pltpu.VMEM((HG_HEADS, HG_DV, HG_DK), F32)],
        compiler_params=_cp(("arbitrary", "arbitrary")),
        name="hgrn2",
    )(z3, z3, z3, z3, lb.reshape(HG_HEADS, 1, HG_DK), norm_w.reshape(1, HG_DV), s0)


def _merge_kernel(x_ref, on_ref, oh_ref, ga_ref, gb_ref, wa_ref, wb_ref, wo_ref, g1_ref, sc_ref, sh_ref,
                  npost_ref, npre_ref, wr_ref, br_ref, y_ref, h_ref, ti_ref, tw_ref):
    a = _dot(on_ref[...].astype(BF16), wa_ref[...])
    bb = _dot(oh_ref[...].astype(BF16), wb_ref[...])
    m = _sigmoid(ga_ref[...]) * a + _sigmoid(gb_ref[...]) * bb
    out = _dot(m.astype(BF16), wo_ref[...])
    y = x_ref[...] + g1_ref[...] * _rms(out, npost_ref[...])
    y_ref[...] = y
    h = _rms(y, npre_ref[...]) * (1.0 + sc_ref[...]) + sh_ref[...]
    h_ref[...] = h
    hh, hl, _ = _split3(h)
    w = wr_ref[...]
    wh = w.astype(BF16)
    wl = (w - wh.astype(F32)).astype(BF16)
    logits = _dot(hh, wh) + _dot(hh, wl) + _dot(hl, wh) + br_ref[...]
    lane = lax.broadcasted_iota(I32, logits.shape, 1)
    lanef = lane.astype(F32)
    logits = jnp.where(lane < N_EXPERTS, logits, -jnp.inf)
    ti = jnp.zeros(logits.shape, F32)
    tv = jnp.zeros(logits.shape, F32)
    v0 = None
    for kk in range(TOP_K):
        mx = jnp.max(logits, axis=-1, keepdims=True)
        idx = jnp.min(jnp.where(logits == mx, lanef, 1e9), axis=-1, keepdims=True)
        v0 = mx if v0 is None else v0
        ti = jnp.where(lane == kk, idx, ti)
        tv = jnp.where(lane == kk, jnp.exp(mx - v0), tv)
        logits = jnp.where(lanef == idx, -jnp.inf, logits)
    ti_ref[...] = ti.astype(I32)
    tw_ref[...] = tv / jnp.sum(tv, axis=-1, keepdims=True)


def _merge(x2d, o_nsa, o_hg, z2d, g1, sc2, sh2, wa, wb, wo, n_post, n_pre, w_router, b_router, rows_per_mod, tm):
    n, d = x2d.shape
    row = lambda w: pl.BlockSpec((tm, w), lambda i: (i, 0))
    if rows_per_mod:
        per = rows_per_mod // tm
        mod_spec = pl.BlockSpec((None, 1, d), lambda i: (i // per, 0, 0))
    else:
        mod_spec = row(d)
    const2 = lambda a: pl.BlockSpec(a.shape, lambda i: (0, 0))
    wr = jnp.pad(w_router, ((0, 0), (0, 128 - N_EXPERTS)))
    br = jnp.pad(b_router, (0, 128 - N_EXPERTS)).reshape(1, 128)
    vec = lambda v: v.reshape(1, d)
    return pl.pallas_call(
        _merge_kernel,
        grid=(n // tm,),
        in_specs=[row(d), row(512), row(512),
                  pl.BlockSpec((tm, d), lambda i: (i, C_GA // 1024)),
                  pl.BlockSpec((tm, d), lambda i: (i, C_GB // 1024)),
                  const2(wa), const2(wb), const2(wo),
                  mod_spec, mod_spec, mod_spec,
                  pl.BlockSpec((1, d), lambda i: (0, 0)), pl.BlockSpec((1, d), lambda i: (0, 0)),
                  const2(wr), const2(br)],
        out_specs=[row(d), row(d), row(128), row(128)],
        out_shape=[jax.ShapeDtypeStruct((n, d), F32), jax.ShapeDtypeStruct((n, d), F32),
                   jax.ShapeDtypeStruct((n, 128), I32), jax.ShapeDtypeStruct((n, 128), F32)],
        compiler_params=_cp(("arbitrary",)),
        name="merge_router",
    )(x2d, o_nsa, o_hg, z2d, z2d, wa, wb, wo, g1, sc2, sh2, vec(n_post), vec(n_pre), wr, br)


def _moe_kernel(wt_ref, we_ref, lo_ref, hi_ref, ni_ref, tok_ref, tok_next_ref, h_hbm,
                wu_ref, bu_ref, wd_ref, bd_ref, o_ref, xbuf, wu16, wd16, sem, *, tm, n_tiles):
    w = pl.program_id(0)
    tile = wt_ref[w]
    prev = jnp.maximum(w - 1, 0)
    first = (w == 0) | (tile != wt_ref[prev])
    valid = w < ni_ref[0]
    slot = tile % 2

    @pl.when(valid & ((w == 0) | (we_ref[w] != we_ref[prev])))
    def _():
        wu16[...] = wu_ref[...].astype(BF16)
        wd16[...] = wd_ref[...].astype(BF16)

    has_next = tile + 1 < n_tiles

    def row_copy(idx_ref, sl, r):
        return pltpu.make_async_copy(h_hbm.at[pl.ds(idx_ref[r], 1), :], xbuf.at[sl, pl.ds(r, 1), :], sem.at[sl])

    @pl.when(valid & (w == 0))
    def _():
        def body(r, c):
            row_copy(tok_ref, 0, r).start()
            return c
        lax.fori_loop(0, tm, body, 0, unroll=8)

    def ffn(issue_next, init):
        if init:
            pltpu.make_async_copy(h_hbm.at[pl.ds(0, tm), :], xbuf.at[slot], sem.at[slot]).wait()
        x = xbuf[slot].astype(BF16)
        if issue_next:
            for r in range(tm):
                row_copy(tok_next_ref, 1 - slot, r).start(priority=r % 2)
        z = _dot(x, wu16[...]) + bu_ref[...]
        de = wd_ref.shape[0]
        gate = jnp.minimum(z[:, :de], SWIGLU_LIMIT)
        up = jnp.clip(z[:, de:], -SWIGLU_LIMIT, SWIGLU_LIMIT)
        act = (up + 1.0) * gate * _sigmoid(SWIGLU_ALPHA * gate)
        y = _dot(act.astype(BF16), wd16[...]) + bd_ref[...]
        row = lax.broadcasted_iota(I32, (tm, 1), 0)
        mine = (row >= lo_ref[w]) & (row < hi_ref[w])
        contrib = jnp.where(mine, y, 0.0)
        o_ref[...] = contrib if init else o_ref[...] + contrib

    @pl.when(valid & first & has_next)
    def _():
        ffn(True, True)

    @pl.when(valid & first & jnp.logical_not(has_next))
    def _():
        ffn(False, True)

    @pl.when(valid & jnp.logical_not(first))
    def _():
        ffn(False, False)


def _moe_ffn(h2, items, row_token, w_up, b_up, w_down, b_down, tm):
    n, d = h2.shape
    n_rows = row_token.shape[0]
    n_tiles = n_rows // tm
    ne, _, dh2 = w_up.shape
    wt, we, lo, hi, ni = items
    grid_spec = pltpu.PrefetchScalarGridSpec(
        num_scalar_prefetch=5,
        grid=(wt.shape[0],),
        in_specs=[pl.BlockSpec((tm,), lambda w, wt, *_: (wt[w],), memory_space=pltpu.SMEM),
                  pl.BlockSpec((tm,), lambda w, wt, *_: (jnp.minimum(wt[w] + 1, n_tiles - 1),), memory_space=pltpu.SMEM),
                  pl.BlockSpec(memory_space=pl.ANY),
                  pl.BlockSpec((None, d, dh2), lambda w, wt, we, *_: (we[w], 0, 0)),
                  pl.BlockSpec((None, 1, dh2), lambda w, wt, we, *_: (we[w], 0, 0)),
                  pl.BlockSpec((None, dh2 // 2, d), lambda w, wt, we, *_: (we[w], 0, 0)),
                  pl.BlockSpec((None, 1, d), lambda w, wt, we, *_: (we[w], 0, 0))],
        out_specs=pl.BlockSpec((tm, d), lambda w, wt, *_: (wt[w], 0)),
        scratch_shapes=[pltpu.VMEM((2, tm, d), F32), pltpu.VMEM((d, dh2), BF16), pltpu.VMEM((dh2 // 2, d), BF16),
                        pltpu.SemaphoreType.DMA((2,))],
    )
    return pl.pallas_call(
        functools.partial(_moe_kernel, tm=tm, n_tiles=n_tiles),
        grid_spec=grid_spec,
        out_shape=jax.ShapeDtypeStruct((n_rows, d), F32),
        compiler_params=_cp(("arbitrary",), 60 * 1024 * 1024),
        name="moe_ffn",
    )(wt, we, lo, hi, ni, row_token, row_token, h2, w_up, b_up.reshape(ne, 1, dh2),
      w_down, b_down.reshape(ne, 1, d))


def _combine_kernel(pos_ref, y_hbm, tw_ref, y1_ref, g2_ref, npost_ref, o_ref, buf, sem, *, tm):
    def body(r, c):
        for kk in range(TOP_K):
            pltpu.make_async_copy(y_hbm.at[pl.ds(pos_ref[r * TOP_K + kk], 1), :], buf.at[kk, pl.ds(r, 1), :],
                                  sem.at[0]).start(priority=kk % 2)
        return c
    lax.fori_loop(0, tm, body, 0, unroll=4)
    for kk in range(TOP_K):
        pltpu.make_async_copy(y_hbm.at[pl.ds(0, tm), :], buf.at[kk], sem.at[0]).wait()
    tw = tw_ref[...]
    moe = ((tw[:, 0:1] * buf[0] + tw[:, 1:2] * buf[1]) + (tw[:, 2:3] * buf[2] + tw[:, 3:4] * buf[3]))
    o_ref[...] = y1_ref[...] + g2_ref[...] * _rms(moe, npost_ref[...])


def _combine(pos_flat, y_sorted, top_w, y1, g2, n_post, rows_per_mod, tm):
    n, d = y1.shape
    if rows_per_mod:
        per = rows_per_mod // tm
        mod_spec = pl.BlockSpec((None, 1, d), lambda i: (i // per, 0, 0))
    else:
        mod_spec = pl.BlockSpec((tm, d), lambda i: (i, 0))
    return pl.pallas_call(
        functools.partial(_combine_kernel, tm=tm),
        grid=(n // tm,),
        in_specs=[pl.BlockSpec((tm * TOP_K,), lambda i: (i,), memory_space=pltpu.SMEM),
                  pl.BlockSpec(memory_space=pl.ANY),
                  pl.BlockSpec((tm, 128), lambda i: (i, 0)),
                  pl.BlockSpec((tm, d), lambda i: (i, 0)),
                  mod_spec,
                  pl.BlockSpec((1, d), lambda i: (0, 0))],
        out_specs=pl.BlockSpec((tm, d), lambda i: (i, 0)),
        out_shape=jax.ShapeDtypeStruct((n, d), F32),
        scratch_shapes=[pltpu.VMEM((TOP_K, tm, d), F32), pltpu.SemaphoreType.DMA((1,))],
        compiler_params=_cp(("arbitrary",)),
        name="moe_combine",
    )(pos_flat, y_sorted, top_w, y1, g2, n_post.reshape(1, d))


def _route_tables(top_i, tm):
    m = top_i.shape[0] * TOP_K
    m_pad = -(-m // tm) * tm
    n_tiles = m_pad // tm
    e_flat = top_i.reshape(-1).astype(I32)
    iota = jnp.arange(m, dtype=I32)
    bits = max(m - 1, 1).bit_length()
    assert N_EXPERTS << bits < 2 ** 31
    order = lax.sort(e_flat * (1 << bits) + iota) & ((1 << bits) - 1)
    _, pos_of = lax.sort((order, iota), num_keys=1)
    row_token = jnp.pad(order // TOP_K, (0, m_pad - m))
    ex = jnp.arange(N_EXPERTS, dtype=I32)
    counts = jnp.sum((e_flat[:, None] == ex[None, :]).astype(I32), axis=0)
    uend = jnp.cumsum(counts)
    ustart = uend - counts
    first_tile = ustart // tm
    n_item_e = jnp.where(counts > 0, (uend - 1) // tm - first_tile + 1, 0)
    iend = jnp.cumsum(n_item_e)
    n_items = iend[-1]
    wid = jnp.arange(n_tiles + N_EXPERTS - 1, dtype=I32)
    we = jnp.minimum(jnp.sum((wid[:, None] >= iend[None, :]).astype(I32), axis=1), N_EXPERTS - 1)
    onehot = (we[:, None] == ex[None, :]).astype(I32)
    pick = lambda v: jnp.sum(onehot * v[None, :], axis=1)
    wt = pick(first_tile) + (wid - pick(iend - n_item_e))
    lo = jnp.maximum(pick(ustart), wt * tm) - wt * tm
    hi = jnp.minimum(pick(uend), (wt + 1) * tm) - wt * tm
    live = wid < n_items
    last_e = jnp.max(jnp.where(counts > 0, ex, 0))
    items = (jnp.where(live, wt, n_tiles - 1), jnp.where(live, we, last_e),
             jnp.where(live, lo, 0), jnp.where(live, hi, 0), n_items.reshape(1))
    return tuple(a.astype(I32) for a in items), row_token, pos_of


def _overlap_table(n_rows, n_cmp, n_cols):
    start = np.arange(n_rows)[:, None] * CMP_STRIDE
    j0 = np.arange(n_cols)[None, :] * SLC_BLOCK
    ov = (start < j0 + SLC_BLOCK) & (start + CMP_BLOCK > j0) & (np.arange(n_rows)[:, None] < n_cmp)
    return jnp.asarray(ov.astype(np.float32), dtype=BF16)


def _expand_table(n_rows, n_keys):
    e = (np.arange(n_keys)[None, :] // SLC_BLOCK) == np.arange(n_rows)[:, None]
    return jnp.asarray(e.astype(np.float32), dtype=BF16)


def _pack_w_in(w_in):
    d = w_in.shape[0]
    q = w_in[:, 0:512]
    kv = w_in[:, 512:1280]
    gn = w_in[:, 1280:1304]
    hh = w_in[:, 1304:3352]
    ga = w_in[:, 3352:4376]
    gb = w_in[:, 4376:5400]
    z = lambda w: jnp.zeros((d, w), w_in.dtype)
    return jnp.concatenate([ga, gb, q, hh, kv, gn, z(ZW - C_GN - 24)], axis=1).astype(BF16)


def _pack_compress(cmp_pe, cmp_w1, cmp_w2):
    r = CMP_BLOCK // CMP_STRIDE
    wt = cmp_w1.reshape(2, r, CMP_STRIDE, DH, CMP_HIDDEN).transpose(0, 2, 3, 1, 4).reshape(2, CMP_STRIDE, DH, r * CMP_HIDDEN)
    zero = jnp.zeros_like(wt)
    w1bd = jnp.concatenate([jnp.concatenate([wt, zero], axis=-1), jnp.concatenate([zero, wt], axis=-1)], axis=2)
    w1bd = w1bd.reshape(2, CMP_STRIDE * 2 * DH, 2 * r * CMP_HIDDEN)
    pe_t = jnp.pad(cmp_pe.reshape(2, r, CMP_STRIDE, DH), ((0, 0), (0, SUB - r), (0, 0), (0, DH)))
    return w1bd.astype(BF16), pe_t.reshape(2, SUB, CMP_STRIDE * 2 * DH), cmp_w2


def _kv_out(z3, c0, rows):
    b = z3.shape[0]
    return z3[:, rows, c0:c0 + 256].reshape(b, -1, 2, NSA_KV, DH)[None]


def kernel(x_prompt, x_sample, c_prompt, c_sample, cache_cmp, cache_sel, state_win, state_hgrn, page_table, w_ada, b_ada, norm_mix_pre, norm_mix_post, norm_ffn_pre, norm_ffn_post, w_in, cmp_pe, cmp_w1, cmp_w2, hg_lb_logits, hg_norm, w_branch_a, w_branch_b, w_out, w_router, b_router, w_up, b_up, w_down, b_down):
    bp, t, d = x_prompt.shape
    bs, ts, _ = x_sample.shape
    depth = w_in.shape[0]
    assert depth == 1 and ts <= SUB
    n_pool = cache_cmp.shape[1]
    past = page_table.shape[1] * PAGE
    lb_all = jnp.cumsum(jax.nn.softmax(hg_lb_logits.astype(F32), axis=0), axis=0)

    l = 0
    w_packed = _pack_w_in(w_in[l])
    w1t, pe_t, w2c = _pack_compress(cmp_pe[l], cmp_w1[l], cmp_w2[l])
    wa, wb, wo = w_branch_a[l].astype(BF16), w_branch_b[l].astype(BF16), w_out[l].astype(BF16)
    wu, wd = w_up[l], w_down[l]

    mod = _modulation(jnp.concatenate([c_prompt, c_sample], axis=0), w_ada[l], b_ada[l])
    mods = [m[:, None, :] for m in jnp.split(mod, 6, axis=-1)]
    sh1, sc1, g1, sh2, sc2, g2 = mods
    p_, s_ = slice(0, bp), slice(bp, bp + bs)

    tm_p = 256 if t % 256 == 0 else t
    zp = _inproj(x_prompt.reshape(bp * t, d), norm_mix_pre[l], sc1[p_], sh1[p_], w_packed, t, tm_p)
    zp3 = zp.reshape(bp, t, ZW)
    nch = t // CMP_STRIDE
    n_slc = -(-t // SLC_BLOCK)
    kvcb_p = _compress_prompt(zp3, w1t, pe_t, w2c)
    tq = 256 if t % 256 == 0 else t
    o_nsa_p = _nsa_prompt(zp3, kvcb_p, _overlap_table(nch, nch - 1, -(-n_slc // SUB) * SUB).T, tq)
    s0_p = jnp.zeros((bp, HG_HEADS, HG_DK, HG_DV), F32)
    chunk = int(np.gcd(t, HG_CHUNK))
    tb_h = 512 if t % 512 == 0 else t
    o_hg_p, hg_state_p = _hgrn(zp3, lb_all[l], hg_norm[l], s0_p, chunk, chunk, tb_h, 16 if chunk % 16 == 0 else chunk)
    tm_m = 512 if t % 512 == 0 else t
    y1_p, h2_p, ti_p, tw_p = _merge(x_prompt.reshape(bp * t, d), o_nsa_p.reshape(bp * t, 512), o_hg_p.reshape(bp * t, 512),
                                    zp, g1[p_], sc2[p_], sh2[p_], wa, wb, wo, norm_mix_post[l], norm_ffn_pre[l],
                                    w_router[l], b_router[l], t, tm_m)

    xs = jnp.pad(x_sample, ((0, 0), (0, SUB - ts), (0, 0))).reshape(bs * SUB, d)
    rep = lambda m: jnp.broadcast_to(m[s_], (bs, SUB, d)).reshape(bs * SUB, d)
    tm_s = 512 if (bs * SUB) % 512 == 0 else bs * SUB
    zs = _inproj(xs, norm_mix_pre[l], rep(sc1), rep(sh1), w_packed, 0, min(tm_s, 256))
    zs3 = zs.reshape(bs, SUB, ZW)
    tok_minor = lambda a: jnp.transpose(a, (0, 2, 3, 4, 1)).reshape(a.shape[0], 2, NSA_KV * DH, a.shape[1])
    o_nsa_s = _nsa_sample(zs3, tok_minor(state_win[l]), tok_minor(cache_cmp[l]), tok_minor(cache_sel[l]),
                          page_table, w1t, pe_t, w2c,
                          _overlap_table(past // CMP_STRIDE, past // CMP_STRIDE - 1, 256),
                          _expand_table(128, past), ts)
    o_hg_s, hg_state_s = _hgrn(zs3, lb_all[l], hg_norm[l], state_hgrn[l], SUB, ts, SUB, SUB)
    y1_s, h2_s, ti_s, tw_s = _merge(xs, o_nsa_s.reshape(bs * SUB, 512), o_hg_s.reshape(bs * SUB, 512), zs,
                                    rep(g1), rep(sc2), rep(sh2), wa, wb, wo, norm_mix_post[l], norm_ffn_pre[l],
                                    w_router[l], b_router[l], 0, tm_s)
    real = lambda a: a.reshape(bs, SUB, -1)[:, :ts].reshape(bs * ts, -1)

    h2 = jnp.concatenate([h2_p, real(h2_s)], axis=0)
    top_i = jnp.concatenate([ti_p[:, :TOP_K], real(ti_s)[:, :TOP_K]], axis=0)
    tm_e = 512
    items, row_token, pos_of = _route_tables(top_i, tm_e)
    y_sorted = _moe_ffn(h2, items, row_token, wu, b_up[l], wd, b_down[l], tm_e)
    np_tok = bp * t
    tm_cp = 256 if t % 256 == 0 else t
    y_p = _combine(pos_of[:np_tok * TOP_K], y_sorted, tw_p, y1_p, g2[p_], norm_ffn_post[l], t, tm_cp).reshape(bp, t, d)
    g2_s = jnp.broadcast_to(g2[s_], (bs, ts, d)).reshape(bs * ts, d)
    tm_cs = 256 if (bs * ts) % 256 == 0 else bs * ts
    y_s = _combine(pos_of[np_tok * TOP_K:], y_sorted, real(tw_s), real(y1_s), g2_s, norm_ffn_post[l], 0,
                   tm_cs).reshape(bs, ts, d)
    keep = min(WINDOW, t)
    new_rows = zs3[:, :ts, C_KW:C_KW + 256].reshape(bs, ts, 2, NSA_KV, DH)
    win_s = jnp.concatenate([state_win[l], new_rows], axis=1)[:, ts:][None]
    return (y_p, y_s,
            _kv_out(zp3, C_KC, slice(0, t)), _kv_out(zs3, C_KC, slice(0, ts)),
            _kv_out(zp3, C_KS, slice(0, t)), _kv_out(zs3, C_KS, slice(0, ts)),
            _kv_out(zp3, C_KW, slice(t - keep, t)), win_s,
            hg_state_p[None], hg_state_s[None])
```

```python
import functools

import numpy as np
import jax
import jax.numpy as jnp
from jax import lax
from jax.experimental import pallas as pl
from jax.experimental.pallas import tpu as pltpu

F32, BF16, I32 = jnp.float32, jnp.bfloat16, jnp.int32

NSA_HEADS, NSA_KV, NSA_GROUP, DH = 8, 2, 4, 64
CMP_BLOCK, CMP_STRIDE, CMP_HIDDEN = 32, 16, 128
SLC_BLOCK, SLC_TOPK, WINDOW = 64, 16, 512
HG_HEADS, HG_DK, HG_DV, HG_CHUNK = 4, 128, 128, 64
N_EXPERTS, TOP_K = 32, 4
SWIGLU_LIMIT, SWIGLU_ALPHA = 7.0, 1.702
EPS, NEG, FORCE = 1e-6, -1e30, 1e4
PAGE = 128

C_GA, C_GB, C_Q, C_HQ, C_HF, C_HI, C_HG = 0, 1024, 2048, 2560, 3072, 3584, 4096
C_KC, C_KS, C_KW, C_GN, ZW = 4608, 4864, 5120, 5376, 5632
SUB = 8

VMEM_LIMIT = 56 * 1024 * 1024


def _cp(sem, vmem=VMEM_LIMIT):
    return pltpu.CompilerParams(dimension_semantics=sem, vmem_limit_bytes=vmem)


def _dot(a, b):
    return jnp.dot(a, b, preferred_element_type=F32)


def _dot_nt(a, b):
    return lax.dot_general(a, b, (((1,), (1,)), ((), ())), preferred_element_type=F32)


def _dot_tn(a, b):
    return lax.dot_general(a, b, (((0,), (0,)), ((), ())), preferred_element_type=F32)


def _split3(x):
    a = x.astype(BF16)
    r = x - a.astype(F32)
    b = r.astype(BF16)
    c = (r - b.astype(F32)).astype(BF16)
    return a, b, c


def _dot3(x, w_bf16):
    a, b, c = _split3(x)
    return _dot(a, w_bf16) + _dot(b, w_bf16) + _dot(c, w_bf16)


def _sigmoid(x):
    return 1.0 / (1.0 + jnp.exp(-x))


def _silu(x):
    return x * _sigmoid(x)


def _rms(x, g):
    return x * lax.rsqrt(jnp.mean(x * x, axis=-1, keepdims=True) + EPS) * g


def _gelu_tanh(x):
    return 0.5 * x * (1.0 + jnp.tanh(0.7978845608028654 * (x + 0.044715 * (x * x * x))))


def _mod_kernel(c_ref, w_ref, b_ref, o_ref):
    o_ref[...] = _dot(_silu(c_ref[...]).astype(BF16), w_ref[...].astype(BF16)) + b_ref[...]


def _modulation(c, w_ada, b_ada):
    m, d = c.shape
    nw = w_ada.shape[1]
    tn = nw // 6
    return pl.pallas_call(
        _mod_kernel,
        grid=(nw // tn,),
        in_specs=[pl.BlockSpec((m, d), lambda j: (0, 0)),
                  pl.BlockSpec((d, tn), lambda j: (0, j)),
                  pl.BlockSpec((1, tn), lambda j: (0, j))],
        out_specs=pl.BlockSpec((m, tn), lambda j: (0, j)),
        out_shape=jax.ShapeDtypeStruct((m, nw), F32),
        compiler_params=_cp(("arbitrary",)),
        name="modulation",
    )(c, w_ada, b_ada.reshape(1, nw))


INPROJ_TN = 512


def _inproj_kernel(x_ref, g_ref, sc_ref, sh_ref, w_ref, o_ref):
    h = (_rms(x_ref[...], g_ref[...]) * (1.0 + sc_ref[...]) + sh_ref[...]).astype(BF16)
    for c0 in range(0, o_ref.shape[1], INPROJ_TN):
        o_ref[:, c0:c0 + INPROJ_TN] = _dot(h, w_ref[:, c0:c0 + INPROJ_TN])


def _inproj(x2d, g, sc, sh, w_packed, rows_per_mod, tm):
    n, d = x2d.shape
    zw = w_packed.shape[1]
    if rows_per_mod:
        per = rows_per_mod // tm
        mod_spec = pl.BlockSpec((None, 1, d), lambda i: (i // per, 0, 0))
    else:
        mod_spec = pl.BlockSpec((tm, d), lambda i: (i, 0))
    return pl.pallas_call(
        _inproj_kernel,
        grid=(n // tm,),
        in_specs=[pl.BlockSpec((tm, d), lambda i: (i, 0)),
                  pl.BlockSpec((1, d), lambda i: (0, 0)),
                  mod_spec, mod_spec,
                  pl.BlockSpec((d, zw), lambda i: (0, 0))],
        out_specs=pl.BlockSpec((tm, zw), lambda i: (i, 0)),
        out_shape=jax.ShapeDtypeStruct((n, zw), F32),
        compiler_params=_cp(("arbitrary",)),
        name="inproj",
    )(x2d, g.reshape(1, d), sc, sh, w_packed)


def _compress(load, nch, w1_ref, pe_ref, w2_ref):
    out = []
    for kv in range(2):
        w1 = w1_ref[kv]
        x = jnp.concatenate([load(kv, tok) for tok in range(CMP_STRIDE)], axis=1).astype(BF16)
        acc = _dot(x, w1)
        pe = _dot(pe_ref[kv].astype(BF16), w1)
        pre0 = pe[0:1, :CMP_HIDDEN] + pe[1:2, CMP_HIDDEN:2 * CMP_HIDDEN]
        w2 = w2_ref[kv].astype(BF16)
        row = []
        for g in range(2):
            c0 = g * 2 * CMP_HIDDEN
            pre = (acc[:, c0:c0 + CMP_HIDDEN]
                   + pltpu.roll(acc[:, c0 + CMP_HIDDEN:c0 + 2 * CMP_HIDDEN], nch - 1, 0) + pre0)
            row.append(_dot(_gelu_tanh(pre).astype(BF16), w2))
        out.append(row)
    return out


def _compress_prompt_kernel(k_ref, v_ref, w1_ref, pe_ref, w2_ref, o_ref, *, nch):
    def load(kv, tok):
        return (k_ref, v_ref)[kv][pl.ds(tok, nch, stride=CMP_STRIDE), :]

    out = _compress(load, nch, w1_ref, pe_ref, w2_ref)
    for kv in range(2):
        for g in range(2):
            o_ref[kv, g] = out[kv][g]


def _compress_prompt(z3, w1t, pe_t, w2):
    b, t, _ = z3.shape
    nch = t // CMP_STRIDE
    return pl.pallas_call(
        functools.partial(_compress_prompt_kernel, nch=nch),
        grid=(b,),
        in_specs=[pl.BlockSpec((None, t, 128), lambda i: (i, 0, C_KC // 128)),
                  pl.BlockSpec((None, t, 128), lambda i: (i, 0, C_KC // 128 + 1)),
                  pl.BlockSpec(w1t.shape, lambda i: (0, 0, 0)),
                  pl.BlockSpec(pe_t.shape, lambda i: (0, 0, 0)),
                  pl.BlockSpec(w2.shape, lambda i: (0, 0, 0))],
        out_specs=pl.BlockSpec((None, 2, 2, nch, DH), lambda i: (i, 0, 0, 0, 0)),
        out_shape=jax.ShapeDtypeStruct((b, 2, 2, nch, DH), F32),
        compiler_params=_cp(("arbitrary",)),
        name="compress_prompt",
    )(z3, z3, w1t, pe_t, w2)


def _slope(g, r):
    return 2.0 ** (-(g * NSA_GROUP + r + 1))


def _stack_heads(q, g):
    parts = [q[:, (g * NSA_GROUP + r) * DH:(g * NSA_GROUP + r + 1) * DH] for r in range(NSA_GROUP)]
    return (jnp.concatenate(parts, axis=0) * (DH ** -0.5)).astype(BF16)


def _head_cols(tq, q0, g):
    pos1 = q0 + lax.broadcasted_iota(I32, (tq, 1), 0)
    qpos = jnp.concatenate([pos1] * NSA_GROUP, axis=0)
    slope = jnp.concatenate([jnp.full((tq, 1), _slope(g, r), F32) for r in range(NSA_GROUP)], axis=0)
    return qpos, -slope * qpos.astype(F32), slope


def _bias_mask(s, kpos_row, valid, a_col, s_col):
    return jnp.where(valid, s + (a_col + s_col * kpos_row.astype(F32)), NEG)


def _masked_scores(q4, k_bf16, kpos_row, valid, a_col, s_col):
    return _bias_mask(_dot_nt(q4, k_bf16), kpos_row, valid, a_col, s_col)


def _softmax_two(s_p, s_n, vt_p, v_n):
    m = jnp.maximum(jnp.max(s_p, axis=-1, keepdims=True), jnp.max(s_n, axis=-1, keepdims=True))
    p_p = jnp.exp(s_p - m)
    p_n = jnp.exp(s_n - m)
    l = jnp.sum(p_p, axis=-1, keepdims=True) + jnp.sum(p_n, axis=-1, keepdims=True)
    return (_dot_nt(p_p.astype(BF16), vt_p) + _dot(p_n.astype(BF16), v_n)) / l


def _topk_mask(score, k, n):
    lane = lax.broadcasted_iota(I32, score.shape, 1)
    ahead = jnp.zeros(score.shape, F32)
    for i in range(n):
        col = score[:, i:i + 1]
        ahead = ahead + jnp.where((col > score) | ((col == score) & (lane > i)), 1.0, 0.0)
    return jnp.where((ahead < k) & (lane < n), 1.0, 0.0)


def _cmp_attend(q4, kcb, vcb, qpos, a_col, s_col, tq):
    ncmp = kcb.shape[0]
    endpos = lax.broadcasted_iota(I32, (1, ncmp), 1) * CMP_STRIDE + (CMP_BLOCK - 1)
    valid = endpos <= qpos
    s = _masked_scores(q4, kcb.astype(BF16), endpos, valid, a_col, s_col)
    m = jnp.max(s, axis=-1, keepdims=True)
    e = jnp.exp(s - m)
    p = jnp.where(valid, e, 0.0) / jnp.sum(e, axis=-1, keepdims=True)
    o = _dot(p.astype(BF16), vcb.astype(BF16))
    return o, p[0:tq] + p[tq:2 * tq] + p[2 * tq:3 * tq] + p[3 * tq:4 * tq]


def _cmp_branch(q4, kcb, vcb, ov_ref, qpos, a_col, s_col, tq, q0, n_slc):
    o, psum = _cmp_attend(q4, kcb, vcb, qpos, a_col, s_col, tq)
    imp = _dot3(psum, ov_ref[...])
    jl = lax.broadcasted_iota(I32, imp.shape, 1)
    cur = lax.shift_right_logical(q0 + lax.broadcasted_iota(I32, (tq, 1), 0), 6)
    forced = (jl == 0) | (jl == cur) | (jl == cur - 1)
    score = jnp.where(forced, FORCE, jnp.where(jl <= cur, imp, -FORCE))
    return o, _topk_mask(score, min(SLC_TOPK, n_slc), n_slc)


def _select_cols(imp, tq, q0, n_slc):
    jr = lax.broadcasted_iota(I32, imp.shape, 0)
    cur = lax.shift_right_logical(q0 + lax.broadcasted_iota(I32, (1, tq), 1), 6)
    forced = (jr == 0) | (jr == cur) | (jr == cur - 1)
    score = jnp.where(forced, FORCE, jnp.where(jr <= cur, imp, -FORCE))
    ahead = jnp.zeros(score.shape, F32)
    for i in range(n_slc):
        rw = score[i:i + 1, :]
        ahead = ahead + jnp.where((rw > score) | ((rw == score) & (jr > i)), 1.0, 0.0)
    return jnp.where((ahead < min(SLC_TOPK, n_slc)) & (jr < n_slc), 1.0, 0.0)


def _online(s, carry, v16):
    m, l, acc = carry
    m_new = jnp.maximum(m, jnp.max(s, axis=-1, keepdims=True))
    alpha = jnp.exp(m - m_new)
    p = jnp.exp(s - m_new)
    return m_new, alpha * l + jnp.sum(p, axis=-1, keepdims=True), alpha * acc + _dot(p.astype(BF16), v16)


def _gate_combine(gn, outs, tq, g):
    sig = _sigmoid(gn)
    heads = []
    for r in range(NSA_GROUP):
        acc = None
        for br, o in enumerate(outs):
            c = br * NSA_HEADS + g * NSA_GROUP + r
            t = sig[:, c:c + 1] * o[r * tq:(r + 1) * tq, :]
            acc = t if acc is None else acc + t
        heads.append(acc)
    return jnp.concatenate(heads, axis=1)


MASK_BIG = 2.0 ** 60
N_SEL_COLS = 32
COL_POS_HI, COL_POS_LO = 32, 33


def _nsa_prompt_kernel(q_ref, gn_ref, kvc_ref, ks_ref, vs_ref, kw_ref, vw_ref, ovt_ref, o_ref,
                       ksa_scr, kwa_scr, vst_scr, vwt_scr, *, tq, t_len, n_slc):
    i = pl.program_id(1)
    q0 = i * tq
    nr = NSA_GROUP * tq

    @pl.when(i == 0)
    def _():
        kp = lax.broadcasted_iota(I32, (t_len, DH), 0)
        ln = lax.broadcasted_iota(I32, (t_len, DH), 1)
        pos_cols = jnp.where(ln == COL_POS_HI, lax.shift_right_logical(kp, 7).astype(F32),
                             jnp.where(ln == COL_POS_LO, (kp & 127).astype(F32), 0.0))
        sel_cols = jnp.where(ln == lax.shift_right_logical(kp, 6), 1.0, pos_cols)
        for g in range(NSA_KV):
            lanes = slice(g * DH, (g + 1) * DH)
            ksa_scr[g] = jnp.concatenate([ks_ref[:, lanes], sel_cols], axis=1).astype(BF16)
            kwa_scr[g] = jnp.concatenate([kw_ref[:, lanes], pos_cols], axis=1).astype(BF16)
        for c0 in range(0, t_len, tq):
            vst_scr[:, c0:c0 + tq] = vs_ref[c0:c0 + tq, :].T.astype(BF16)
            vwt_scr[:, c0:c0 + tq] = vw_ref[c0:c0 + tq, :].T.astype(BF16)

    lane = lax.broadcasted_iota(I32, (1, nr), 1)
    qpos = q0 + (lane & (tq - 1))
    head = lax.shift_right_logical(lane, tq.bit_length() - 1)
    krow = lax.broadcasted_iota(I32, (tq, 1), 0)
    sig_t = _sigmoid(gn_ref[...].T)
    init = (jnp.full((1, nr), NEG, F32), jnp.zeros((1, nr), F32), jnp.zeros((DH, nr), F32))
    qas, o_cmps = [], []
    for g in range(NSA_KV):
        slope = jnp.where(head == 0, _slope(g, 0), jnp.where(head == 1, _slope(g, 1),
                          jnp.where(head == 2, _slope(g, 2), _slope(g, 3))))
        qt = q_ref[:, g * 256:(g + 1) * 256].T * (DH ** -0.5)
        q4t = jnp.concatenate([qt[r * DH:(r + 1) * DH, :] for r in range(NSA_GROUP)], axis=1)

        ncmp = kvc_ref.shape[2]
        endpos = lax.broadcasted_iota(I32, (ncmp, 1), 0) * CMP_STRIDE + (CMP_BLOCK - 1)
        valid = endpos <= qpos
        s = _dot(kvc_ref[0, g].astype(BF16), q4t.astype(BF16)) + slope * endpos.astype(F32)
        s = jnp.where(valid, s, NEG)
        e = jnp.exp(s - jnp.max(s, axis=0, keepdims=True))
        p = jnp.where(valid, e, 0.0) / jnp.sum(e, axis=0, keepdims=True)
        o_cmp = _dot_tn(kvc_ref[1, g].astype(BF16), p.astype(BF16))
        psum_t = p[:, 0:tq] + p[:, tq:2 * tq] + p[:, 2 * tq:3 * tq] + p[:, 3 * tq:4 * tq]
        imp_t = sum(_dot(ovt_ref[...], part) for part in _split3(psum_t))
        sel_t = _select_cols(imp_t, tq, q0, n_slc)
        sel4 = jnp.concatenate([sel_t] * NSA_GROUP, axis=1)
        jr = lax.broadcasted_iota(I32, (N_SEL_COLS, nr), 0)
        pos_rows = jnp.where(jr == COL_POS_HI - N_SEL_COLS, 128.0 * slope,
                             jnp.where(jr == COL_POS_LO - N_SEL_COLS, slope, 0.0))
        pieces = [q4t, (sel4 - 1.0) * MASK_BIG]
        if sel4.shape[0] < N_SEL_COLS:
            pieces.append(jnp.zeros((N_SEL_COLS - sel4.shape[0], nr), F32))
        qas.append(jnp.concatenate(pieces + [pos_rows], axis=0).astype(BF16))
        o_cmps.append(o_cmp)

    def attend(ka_ref, vt_ref, kt_lo, band):
        def tile(kt, carries, diag):
            k0 = pl.multiple_of(kt * tq, tq)
            out = []
            for g in range(NSA_KV):
                s = _dot(ka_ref[g, pl.ds(k0, tq), :], qas[g])
                if band:
                    s = jnp.where(k0 + krow > qpos - WINDOW, s, NEG)
                if diag:
                    s = jnp.where(k0 + krow <= qpos, s, NEG)
                m, l, acc = carries[g]
                m_new = jnp.maximum(m, jnp.max(s, axis=0, keepdims=True))
                alpha = jnp.exp(m - m_new)
                pt = jnp.exp(s - m_new)
                l = alpha * l + jnp.sum(pt, axis=0, keepdims=True)
                acc = alpha * acc + _dot(vt_ref[g * DH:(g + 1) * DH, pl.ds(k0, tq)], pt.astype(BF16))
                out.append((m_new, l, acc))
            return tuple(out)

        carries = lax.fori_loop(kt_lo, i, lambda kt, c: tile(kt, c, False), (init,) * NSA_KV)
        return [acc / l for _, l, acc in tile(i, carries, True)]

    o_slcs = attend(ksa_scr, vst_scr, 0, False)
    o_wins = attend(kwa_scr, vwt_scr, jnp.maximum(i - WINDOW // tq, 0), True)

    for g in range(NSA_KV):
        o_cmp, o_slc, o_win = o_cmps[g], o_slcs[g], o_wins[g]
        heads = []
        for r in range(NSA_GROUP):
            cols = slice(r * tq, (r + 1) * tq)
            acc = None
            for br, o in enumerate((o_cmp, o_slc, o_win)):
                c = br * NSA_HEADS + g * NSA_GROUP + r
                t = sig_t[c:c + 1, :] * o[:, cols]
                acc = t if acc is None else acc + t
            heads.append(acc)
        o_ref[:, g * 256:(g + 1) * 256] = jnp.concatenate(heads, axis=0).T


def _nsa_prompt(z3, kvcb, ovt, tq):
    b, t, _ = z3.shape
    n_slc = -(-t // SLC_BLOCK)
    assert n_slc <= N_SEL_COLS and WINDOW % tq == 0 and t % tq == 0 and tq & (tq - 1) == 0
    kv_spec = lambda col: pl.BlockSpec((None, t, 128), lambda bi, i, c=col // 128: (bi, 0, c))
    return pl.pallas_call(
        functools.partial(_nsa_prompt_kernel, tq=tq, t_len=t, n_slc=n_slc),
        grid=(b, t // tq),
        in_specs=[pl.BlockSpec((None, tq, 512), lambda bi, i: (bi, i, C_Q // 512)),
                  pl.BlockSpec((None, tq, 128), lambda bi, i: (bi, i, C_GN // 128)),
                  pl.BlockSpec((None,) + kvcb.shape[1:], lambda bi, i: (bi, 0, 0, 0, 0)),
                  kv_spec(C_KS), kv_spec(C_KS + 128), kv_spec(C_KW), kv_spec(C_KW + 128),
                  pl.BlockSpec(ovt.shape, lambda bi, i: (0, 0))],
        out_specs=pl.BlockSpec((None, tq, 512), lambda bi, i: (bi, i, 0)),
        out_shape=jax.ShapeDtypeStruct((b, t, 512), F32),
        scratch_shapes=[pltpu.VMEM((NSA_KV, t, 2 * DH), BF16), pltpu.VMEM((NSA_KV, t, 2 * DH), BF16),
                        pltpu.VMEM((2 * DH, t), BF16), pltpu.VMEM((2 * DH, t), BF16)],
        compiler_params=_cp(("arbitrary", "arbitrary")),
        name="nsa_prompt",
    )(z3, z3, kvcb, z3, z3, z3, z3, ovt)


def _nsa_sample_kernel(pt_ref, q_ref, gn_ref, kvs_ref, kvw_ref, win_ref, cmp_hbm, sel_hbm,
                       w1_ref, pe_ref, w2_ref, ov_ref, e_ref, o_ref,
                       cmp_buf, sel_buf, rows_scr, new_scr, sem, *, n_pages, past, t_new):
    b = pl.program_id(0)
    nb = pl.num_programs(0)
    slot = b % 2
    nch = past // CMP_STRIDE
    wb = win_ref.shape[-1]

    def page_copies(bb, sl, pg):
        toks = pl.ds(pg * PAGE, PAGE)
        p = pt_ref[bb, pg]
        return (pltpu.make_async_copy(cmp_hbm.at[p], cmp_buf.at[sl, :, :, toks], sem.at[0, sl]),
                pltpu.make_async_copy(sel_hbm.at[p], sel_buf.at[sl, :, :, toks], sem.at[1, sl]))

    def fetch(bb, sl):
        for pg in range(n_pages):
            for c in page_copies(bb, sl, pg):
                c.start()

    @pl.when(b == 0)
    def _():
        fetch(0, 0)

    @pl.when(b + 1 < nb)
    def _():
        fetch(b + 1, 1 - slot)

    for pg in range(n_pages):
        for c in page_copies(b, slot, pg):
            c.wait()

    new_scr[...] = jnp.zeros(new_scr.shape, F32)
    new_scr[0:SUB, 0:256] = kvs_ref[...]
    new_scr[0:SUB, 256:512] = kvw_ref[...]

    tblk = 512 if past % 512 == 0 else past

    def load(kv, tok):
        if tok == 0:
            for c0 in range(0, past, tblk):
                rows_scr[c0:c0 + tblk, :] = cmp_buf[slot, kv, :, c0:c0 + tblk].T
        return rows_scr[pl.ds(tok, nch, stride=CMP_STRIDE), :]

    kvcb = _compress(load, nch, w1_ref, pe_ref, w2_ref)

    tq = SUB
    n_slc = -(-(past + t_new) // SLC_BLOCK)
    q = q_ref[...]
    gn = gn_ref[...]
    npos = past + lax.broadcasted_iota(I32, (1, 128), 1)
    ppos = lax.broadcasted_iota(I32, (1, past), 1)
    wpos = past - wb + lax.broadcasted_iota(I32, (1, wb), 1)
    for g in range(NSA_KV):
        rows = slice(g * DH, (g + 1) * DH)
        q4 = _stack_heads(q, g)
        qpos, a_col, s_col = _head_cols(tq, past, g)
        o_cmp, sel = _cmp_branch(q4, kvcb[0][g], kvcb[1][g], ov_ref, qpos, a_col, s_col, tq, past, n_slc)
        sel4 = jnp.concatenate([sel[:, :128]] * NSA_GROUP, axis=0).astype(BF16)

        picked = _dot(sel4, e_ref[...]) > 0.5
        s_p = _bias_mask(_dot(q4, sel_buf[slot, 0, rows, :].astype(BF16)), ppos, picked & (ppos <= qpos), a_col, s_col)
        s_n = _masked_scores(q4, new_scr[:, rows].astype(BF16), npos, npos <= qpos, a_col, s_col)
        o_slc = _softmax_two(s_p, s_n, sel_buf[slot, 1, rows, :].astype(BF16),
                             new_scr[:, 128 + g * DH:128 + (g + 1) * DH].astype(BF16))

        s_p = _bias_mask(_dot(q4, win_ref[0, rows, :].astype(BF16)), wpos, (wpos <= qpos) & (wpos > qpos - WINDOW),
                         a_col, s_col)
        s_n = _masked_scores(q4, new_scr[:, 256 + g * DH:256 + (g + 1) * DH].astype(BF16), npos, npos <= qpos,
                             a_col, s_col)
        o_win = _softmax_two(s_p, s_n, win_ref[1, rows, :].astype(BF16),
                             new_scr[:, 384 + g * DH:384 + (g + 1) * DH].astype(BF16))

        o_ref[:, g * 256:(g + 1) * 256] = _gate_combine(gn, (o_cmp, o_slc, o_win), tq, g)


def _nsa_sample(z3, win, cache_cmp, cache_sel, page_table, w1t, pe_t, w2, ov, e_mat, t_new):
    bs = z3.shape[0]
    n_pages = page_table.shape[1]
    past = n_pages * PAGE
    wb = win.shape[-1]
    const = lambda shape: pl.BlockSpec(shape, lambda bi, pt, n=len(shape): (0,) * n)
    grid_spec = pltpu.PrefetchScalarGridSpec(
        num_scalar_prefetch=1,
        grid=(bs,),
        in_specs=[pl.BlockSpec((None, SUB, 512), lambda bi, pt: (bi, 0, C_Q // 512)),
                  pl.BlockSpec((None, SUB, 128), lambda bi, pt: (bi, 0, C_GN // 128)),
                  pl.BlockSpec((None, SUB, 256), lambda bi, pt: (bi, 0, C_KS // 256)),
                  pl.BlockSpec((None, SUB, 256), lambda bi, pt: (bi, 0, C_KW // 256)),
                  pl.BlockSpec((None, 2, 128, wb), lambda bi, pt: (bi, 0, 0, 0)),
                  pl.BlockSpec(memory_space=pl.ANY),
                  pl.BlockSpec(memory_space=pl.ANY),
                  const(w1t.shape), const(pe_t.shape), const(w2.shape), const(ov.shape), const(e_mat.shape)],
        out_specs=pl.BlockSpec((None, SUB, 512), lambda bi, pt: (bi, 0, 0)),
        scratch_shapes=[pltpu.VMEM((2, 2, 128, past), F32),
                        pltpu.VMEM((2, 2, 128, past), F32),
                        pltpu.VMEM((past, 128), F32),
                        pltpu.VMEM((128, 512), F32),
                        pltpu.SemaphoreType.DMA((2, 2))],
    )
    return pl.pallas_call(
        functools.partial(_nsa_sample_kernel, n_pages=n_pages, past=past, t_new=t_new),
        grid_spec=grid_spec,
        out_shape=jax.ShapeDtypeStruct((bs, SUB, 512), F32),
        compiler_params=_cp(("arbitrary",), 60 * 1024 * 1024),
        name="nsa_sample",
    )(page_table, z3, z3, z3, z3, win, cache_cmp, cache_sel, w1t, pe_t, w2, ov, e_mat)


def _hgrn_kernel(hq_ref, hf_ref, hi_ref, hg_ref, lb_ref, nw_ref, s0_ref, o_ref, sout_ref, st_scr,
                 *, tb, chunk, t_real, blk):
    j = pl.program_id(1)
    nw = nw_ref[...]
    row = lax.broadcasted_iota(I32, (chunk, 1), 0)
    real = row < t_real
    tri = (lax.broadcasted_iota(I32, (chunk, chunk), 0) >= lax.broadcasted_iota(I32, (chunk, chunk), 1)).astype(BF16)
    n_rows = min(chunk, t_real)

    @pl.when(j == 0)
    def _():
        for h in range(HG_HEADS):
            st_scr[h] = s0_ref[h].T

    def head_chunk(rows, h):
        cols = slice(h * HG_DK, (h + 1) * HG_DK)
        lb = lb_ref[h]
        q = _silu(hq_ref[rows, cols])
        f = lb + (1.0 - lb) * _sigmoid(hf_ref[rows, cols])
        k = jnp.where(real, 1.0 - f, 0.0)
        gl = jnp.where(real, jnp.log(f), 0.0)
        v = hi_ref[rows, cols]
        if chunk > 8:
            ga_, gb_, gc_ = _split3(gl)
            bcum = _dot(tri, ga_) + _dot(tri, gb_) + _dot(tri, gc_)
        else:
            bcum = jnp.zeros_like(gl)
            for s in range(t_real):
                bcum = bcum + jnp.where(row >= s, gl[s:s + 1, :], 0.0)
        st = st_scr[h]
        o_inter = _dot_nt((q * jnp.exp(bcum)).astype(BF16), st.astype(BF16))
        v16 = v.astype(BF16)
        parts = []
        for lo in range(0, chunk, blk):
            hi = min(lo + blk, chunk)
            acc = o_inter[lo:hi, :]
            if lo < n_rows:
                qj = q[lo:hi, :]
                bj = bcum[lo:hi, :]
                tj = row[lo:hi, :]
                for s in range(lo, min(hi, t_real)):
                    d = jnp.where(tj >= s, bj - bcum[s:s + 1, :], NEG)
                    w = jnp.sum(qj * jnp.exp(d) * k[s:s + 1, :], axis=-1, keepdims=True)
                    acc = acc + w * v[s:s + 1, :]
                if lo > 0:
                    ref = bcum[lo - 1:lo, :]
                    qt = (qj * jnp.exp(bj - ref)).astype(BF16)
                    kt = (k[0:lo, :] * jnp.exp(ref - bcum[0:lo, :])).astype(BF16)
                    acc = acc + _dot(_dot_nt(qt, kt).astype(BF16), v16[0:lo, :])
            parts.append(acc)
        o = parts[0] if len(parts) == 1 else jnp.concatenate(parts, axis=0)
        bc = bcum[chunk - 1:chunk, :]
        kt = k * jnp.exp(bc - bcum)
        st_scr[h] = st * jnp.exp(bc) + _dot_tn(v16, kt.astype(BF16))
        o = o * lax.rsqrt(jnp.mean(o * o, axis=-1, keepdims=True) + EPS) * nw
        o_ref[rows, cols] = o * _silu(hg_ref[rows, cols])

    def step(ci, carry):
        rows = pl.ds(pl.multiple_of(ci * chunk, chunk), chunk)
        for h in range(HG_HEADS):
            head_chunk(rows, h)
        return carry

    lax.fori_loop(0, tb // chunk, step, 0, unroll=True)

    @pl.when(j == pl.num_programs(1) - 1)
    def _():
        for h in range(HG_HEADS):
            sout_ref[h] = st_scr[h].T


def _hgrn(z3, lb, norm_w, s0, chunk, t_real, tb, blk):
    b, t, _ = z3.shape
    w = HG_HEADS * HG_DK
    col = lambda c0: pl.BlockSpec((None, tb, w), lambda bi, j, c=c0 // w: (bi, j, c))
    state = pl.BlockSpec((None, HG_HEADS, HG_DK, HG_DV), lambda bi, j: (bi, 0, 0, 0))
    return pl.pallas_call(
        functools.partial(_hgrn_kernel, tb=tb, chunk=chunk, t_real=t_real, blk=blk),
        grid=(b, t // tb),
        in_specs=[col(C_HQ), col(C_HF), col(C_HI), col(C_HG),
                  pl.BlockSpec((HG_HEADS, 1, HG_DK), lambda bi, j: (0, 0, 0)),
                  pl.BlockSpec((1, HG_DV), lambda bi, j: (0, 0)),
                  state],
        out_specs=[pl.BlockSpec((None, tb, w), lambda bi, j: (bi, j, 0)), state],
        out_shape=[jax.ShapeDtypeStruct((b, t, w), F32),
                   jax.ShapeDtypeStruct((b, HG_HEADS, HG_DK, HG_DV), F32)],
        scratch_shapes=[pltpu.VMEM((HG_HEADS, HG_DV, HG_DK), F32)],
        compiler_params=_cp(("arbitrary", "arbitrary")),
        name="hgrn2",
    )(z3, z3, z3, z3, lb.reshape(HG_HEADS, 1, HG_DK), norm_w.reshape(1, HG_DV), s0)


def _merge_kernel(x_ref, on_ref, oh_ref, ga_ref, gb_ref, wa_ref, wb_ref, wo_ref, g1_ref, sc_ref, sh_ref,
                  npost_ref, npre_ref, wr_ref, br_ref, y_ref, h_ref, ti_ref, tw_ref):
    a = _dot(on_ref[...].astype(BF16), wa_ref[...])
    bb = _dot(oh_ref[...].astype(BF16), wb_ref[...])
    m = _sigmoid(ga_ref[...]) * a + _sigmoid(gb_ref[...]) * bb
    out = _dot(m.astype(BF16), wo_ref[...])
    y = x_ref[...] + g1_ref[...] * _rms(out, npost_ref[...])
    y_ref[...] = y
    h = _rms(y, npre_ref[...]) * (1.0 + sc_ref[...]) + sh_ref[...]
    h_ref[...] = h
    hh, hl, _ = _split3(h)
    w = wr_ref[...]
    wh = w.astype(BF16)
    wl = (w - wh.astype(F32)).astype(BF16)
    logits = _dot(hh, wh) + _dot(hh, wl) + _dot(hl, wh) + br_ref[...]
    lane = lax.broadcasted_iota(I32, logits.shape, 1)
    lanef = lane.astype(F32)
    logits = jnp.where(lane < N_EXPERTS, logits, -jnp.inf)
    ti = jnp.zeros(logits.shape, F32)
    tv = jnp.zeros(logits.shape, F32)
    v0 = None
    for kk in range(TOP_K):
        mx = jnp.max(logits, axis=-1, keepdims=True)
        idx = jnp.min(jnp.where(logits == mx, lanef, 1e9), axis=-1, keepdims=True)
        v0 = mx if v0 is None else v0
        ti = jnp.where(lane == kk, idx, ti)
        tv = jnp.where(lane == kk, jnp.exp(mx - v0), tv)
        logits = jnp.where(lanef == idx, -jnp.inf, logits)
    ti_ref[...] = ti.astype(I32)
    tw_ref[...] = tv / jnp.sum(tv, axis=-1, keepdims=True)


def _merge(x2d, o_nsa, o_hg, z2d, g1, sc2, sh2, wa, wb, wo, n_post, n_pre, w_router, b_router, rows_per_mod, tm):
    n, d = x2d.shape
    row = lambda w: pl.BlockSpec((tm, w), lambda i: (i, 0))
    if rows_per_mod:
        per = rows_per_mod // tm
        mod_spec = pl.BlockSpec((None, 1, d), lambda i: (i // per, 0, 0))
    else:
        mod_spec = row(d)
    const2 = lambda a: pl.BlockSpec(a.shape, lambda i: (0, 0))
    wr = jnp.pad(w_router, ((0, 0), (0, 128 - N_EXPERTS)))
    br = jnp.pad(b_router, (0, 128 - N_EXPERTS)).reshape(1, 128)
    vec = lambda v: v.reshape(1, d)
    return pl.pallas_call(
        _merge_kernel,
        grid=(n // tm,),
        in_specs=[row(d), row(512), row(512),
                  pl.BlockSpec((tm, d), lambda i: (i, C_GA // 1024)),
                  pl.BlockSpec((tm, d), lambda i: (i, C_GB // 1024)),
                  const2(wa), const2(wb), const2(wo),
                  mod_spec, mod_spec, mod_spec,
                  pl.BlockSpec((1, d), lambda i: (0, 0)), pl.BlockSpec((1, d), lambda i: (0, 0)),
                  const2(wr), const2(br)],
        out_specs=[row(d), row(d), row(128), row(128)],
        out_shape=[jax.ShapeDtypeStruct((n, d), F32), jax.ShapeDtypeStruct((n, d), F32),
                   jax.ShapeDtypeStruct((n, 128), I32), jax.ShapeDtypeStruct((n, 128), F32)],
        compiler_params=_cp(("arbitrary",)),
        name="merge_router",
    )(x2d, o_nsa, o_hg, z2d, z2d, wa, wb, wo, g1, sc2, sh2, vec(n_post), vec(n_pre), wr, br)


N_XBUF = 3


def _moe_kernel(wt_ref, we_ref, lo_ref, hi_ref, ni_ref, tok_ref, tok_next_ref, tok_next2_ref, h_hbm,
                wu_ref, bu_ref, wd_ref, bd_ref, o_ref, xbuf, wu16, wd16, sem, *, tm, n_tiles):
    w = pl.program_id(0)
    tile = wt_ref[w]
    prev = jnp.maximum(w - 1, 0)
    first = (w == 0) | (tile != wt_ref[prev])
    valid = w < ni_ref[0]
    slot = lax.rem(tile, N_XBUF)
    slot_next = lax.rem(tile + (N_XBUF - 1), N_XBUF)

    @pl.when(valid & ((w == 0) | (we_ref[w] != we_ref[prev])))
    def _():
        wu16[...] = wu_ref[...].astype(BF16)
        wd16[...] = wd_ref[...].astype(BF16)

    has_next = tile + (N_XBUF - 1) < n_tiles

    def row_copy(idx_ref, sl, r):
        return pltpu.make_async_copy(h_hbm.at[pl.ds(idx_ref[r], 1), :], xbuf.at[sl, pl.ds(r, 1), :], sem.at[sl])

    def gather_loop(idx_ref, sl):
        def body(r, c):
            row_copy(idx_ref, sl, r).start()
            return c
        lax.fori_loop(0, tm, body, 0, unroll=8)

    @pl.when(valid & (w == 0))
    def _():
        gather_loop(tok_ref, 0)

    @pl.when(valid & (w == 0) & (n_tiles > 1))
    def _():
        gather_loop(tok_next_ref, 1)

    def ffn(issue_next, init):
        if init:
            pltpu.make_async_copy(h_hbm.at[pl.ds(0, tm), :], xbuf.at[slot], sem.at[slot]).wait()
        x = xbuf[slot].astype(BF16)
        if issue_next:
            for r in range(tm):
                row_copy(tok_next2_ref, slot_next, r).start(priority=r % 2)
        z = _dot(x, wu16[...]) + bu_ref[...]
        de = wd_ref.shape[0]
        gate = jnp.minimum(z[:, :de], SWIGLU_LIMIT)
        up = jnp.clip(z[:, de:], -SWIGLU_LIMIT, SWIGLU_LIMIT)
        act = (up + 1.0) * gate * _sigmoid(SWIGLU_ALPHA * gate)
        y = _dot(act.astype(BF16), wd16[...]) + bd_ref[...]
        row = lax.broadcasted_iota(I32, (tm, 1), 0)
        mine = (row >= lo_ref[w]) & (row < hi_ref[w])
        contrib = jnp.where(mine, y, 0.0)
        o_ref[...] = contrib if init else o_ref[...] + contrib

    @pl.when(valid & first & has_next)
    def _():
        ffn(True, True)

    @pl.when(valid & first & jnp.logical_not(has_next))
    def _():
        ffn(False, True)

    @pl.when(valid & jnp.logical_not(first))
    def _():
        ffn(False, False)


def _moe_ffn(h2, items, row_token, w_up, b_up, w_down, b_down, tm):
    n, d = h2.shape
    n_rows = row_token.shape[0]
    n_tiles = n_rows // tm
    ne, _, dh2 = w_up.shape
    wt, we, lo, hi, ni = items
    grid_spec = pltpu.PrefetchScalarGridSpec(
        num_scalar_prefetch=5,
        grid=(wt.shape[0],),
        in_specs=[pl.BlockSpec((tm,), lambda w, wt, *_: (wt[w],), memory_space=pltpu.SMEM),
                  pl.BlockSpec((tm,), lambda w, wt, *_: (jnp.minimum(wt[w] + 1, n_tiles - 1),), memory_space=pltpu.SMEM),
                  pl.BlockSpec((tm,), lambda w, wt, *_: (jnp.minimum(wt[w] + 2, n_tiles - 1),), memory_space=pltpu.SMEM),
                  pl.BlockSpec(memory_space=pl.ANY),
                  pl.BlockSpec((None, d, dh2), lambda w, wt, we, *_: (we[w], 0, 0)),
                  pl.BlockSpec((None, 1, dh2), lambda w, wt, we, *_: (we[w], 0, 0)),
                  pl.BlockSpec((None, dh2 // 2, d), lambda w, wt, we, *_: (we[w], 0, 0)),
                  pl.BlockSpec((None, 1, d), lambda w, wt, we, *_: (we[w], 0, 0))],
        out_specs=pl.BlockSpec((tm, d), lambda w, wt, *_: (wt[w], 0)),
        scratch_shapes=[pltpu.VMEM((N_XBUF, tm, d), F32), pltpu.VMEM((d, dh2), BF16), pltpu.VMEM((dh2 // 2, d), BF16),
                        pltpu.SemaphoreType.DMA((N_XBUF,))],
    )
    return pl.pallas_call(
        functools.partial(_moe_kernel, tm=tm, n_tiles=n_tiles),
        grid_spec=grid_spec,
        out_shape=jax.ShapeDtypeStruct((n_rows, d), F32),
        compiler_params=_cp(("arbitrary",), 60 * 1024 * 1024),
        name="moe_ffn",
    )(wt, we, lo, hi, ni, row_token, row_token, row_token, h2, w_up, b_up.reshape(ne, 1, dh2),
      w_down, b_down.reshape(ne, 1, d))


def _combine_kernel(pos_ref, y_hbm, tw_ref, y1_ref, g2_ref, npost_ref, o_ref, buf, sem, *, tm):
    def body(r, c):
        for kk in range(TOP_K):
            pltpu.make_async_copy(y_hbm.at[pl.ds(pos_ref[r * TOP_K + kk], 1), :], buf.at[kk, pl.ds(r, 1), :],
                                  sem.at[0]).start(priority=kk % 2)
        return c
    lax.fori_loop(0, tm, body, 0, unroll=4)
    for kk in range(TOP_K):
        pltpu.make_async_copy(y_hbm.at[pl.ds(0, tm), :], buf.at[kk], sem.at[0]).wait()
    tw = tw_ref[...]
    moe = ((tw[:, 0:1] * buf[0] + tw[:, 1:2] * buf[1]) + (tw[:, 2:3] * buf[2] + tw[:, 3:4] * buf[3]))
    o_ref[...] = y1_ref[...] + g2_ref[...] * _rms(moe, npost_ref[...])


def _combine(pos_flat, y_sorted, top_w, y1, g2, n_post, rows_per_mod, tm):
    n, d = y1.shape
    if rows_per_mod:
        per = rows_per_mod // tm
        mod_spec = pl.BlockSpec((None, 1, d), lambda i: (i // per, 0, 0))
    else:
        mod_spec = pl.BlockSpec((tm, d), lambda i: (i, 0))
    return pl.pallas_call(
        functools.partial(_combine_kernel, tm=tm),
        grid=(n // tm,),
        in_specs=[pl.BlockSpec((tm * TOP_K,), lambda i: (i,), memory_space=pltpu.SMEM),
                  pl.BlockSpec(memory_space=pl.ANY),
                  pl.BlockSpec((tm, 128), lambda i: (i, 0)),
                  pl.BlockSpec((tm, d), lambda i: (i, 0)),
                  mod_spec,
                  pl.BlockSpec((1, d), lambda i: (0, 0))],
        out_specs=pl.BlockSpec((tm, d), lambda i: (i, 0)),
        out_shape=jax.ShapeDtypeStruct((n, d), F32),
        scratch_shapes=[pltpu.VMEM((TOP_K, tm, d), F32), pltpu.SemaphoreType.DMA((1,))],
        compiler_params=_cp(("arbitrary",)),
        name="moe_combine",
    )(pos_flat, y_sorted, top_w, y1, g2, n_post.reshape(1, d))


def _route_tables(top_i, tm):
    m = top_i.shape[0] * TOP_K
    m_pad = -(-m // tm) * tm
    n_tiles = m_pad // tm
    e_flat = top_i.reshape(-1).astype(I32)
    iota = jnp.arange(m, dtype=I32)
    bits = max(m - 1, 1).bit_length()
    assert N_EXPERTS << bits < 2 ** 31
    order = lax.sort(e_flat * (1 << bits) + iota) & ((1 << bits) - 1)
    _, pos_of = lax.sort((order, iota), num_keys=1)
    row_token = jnp.pad(order // TOP_K, (0, m_pad - m))
    ex = jnp.arange(N_EXPERTS, dtype=I32)
    counts = jnp.sum((e_flat[:, None] == ex[None, :]).astype(I32), axis=0)
    uend = jnp.cumsum(counts)
    ustart = uend - counts
    first_tile = ustart // tm
    n_item_e = jnp.where(counts > 0, (uend - 1) // tm - first_tile + 1, 0)
    iend = jnp.cumsum(n_item_e)
    n_items = iend[-1]
    wid = jnp.arange(n_tiles + N_EXPERTS - 1, dtype=I32)
    we = jnp.minimum(jnp.sum((wid[:, None] >= iend[None, :]).astype(I32), axis=1), N_EXPERTS - 1)
    onehot = (we[:, None] == ex[None, :]).astype(I32)
    pick = lambda v: jnp.sum(onehot * v[None, :], axis=1)
    wt = pick(first_tile) + (wid - pick(iend - n_item_e))
    lo = jnp.maximum(pick(ustart), wt * tm) - wt * tm
    hi = jnp.minimum(pick(uend), (wt + 1) * tm) - wt * tm
    live = wid < n_items
    last_e = jnp.max(jnp.where(counts > 0, ex, 0))
    items = (jnp.where(live, wt, n_tiles - 1), jnp.where(live, we, last_e),
             jnp.where(live, lo, 0), jnp.where(live, hi, 0), n_items.reshape(1))
    return tuple(a.astype(I32) for a in items), row_token, pos_of


def _overlap_table(n_rows, n_cmp, n_cols):
    start = np.arange(n_rows)[:, None] * CMP_STRIDE
    j0 = np.arange(n_cols)[None, :] * SLC_BLOCK
    ov = (start < j0 + SLC_BLOCK) & (start + CMP_BLOCK > j0) & (np.arange(n_rows)[:, None] < n_cmp)
    return jnp.asarray(ov.astype(np.float32), dtype=BF16)


def _expand_table(n_rows, n_keys):
    e = (np.arange(n_keys)[None, :] // SLC_BLOCK) == np.arange(n_rows)[:, None]
    return jnp.asarray(e.astype(np.float32), dtype=BF16)


def _pack_w_in(w_in):
    d = w_in.shape[0]
    q = w_in[:, 0:512]
    kv = w_in[:, 512:1280]
    gn = w_in[:, 1280:1304]
    hh = w_in[:, 1304:3352]
    ga = w_in[:, 3352:4376]
    gb = w_in[:, 4376:5400]
    z = lambda w: jnp.zeros((d, w), w_in.dtype)
    return jnp.concatenate([ga, gb, q, hh, kv, gn, z(ZW - C_GN - 24)], axis=1).astype(BF16)


def _pack_compress(cmp_pe, cmp_w1, cmp_w2):
    r = CMP_BLOCK // CMP_STRIDE
    wt = cmp_w1.reshape(2, r, CMP_STRIDE, DH, CMP_HIDDEN).transpose(0, 2, 3, 1, 4).reshape(2, CMP_STRIDE, DH, r * CMP_HIDDEN)
    zero = jnp.zeros_like(wt)
    w1bd = jnp.concatenate([jnp.concatenate([wt, zero], axis=-1), jnp.concatenate([zero, wt], axis=-1)], axis=2)
    w1bd = w1bd.reshape(2, CMP_STRIDE * 2 * DH, 2 * r * CMP_HIDDEN)
    pe_t = jnp.pad(cmp_pe.reshape(2, r, CMP_STRIDE, DH), ((0, 0), (0, SUB - r), (0, 0), (0, DH)))
    return w1bd.astype(BF16), pe_t.reshape(2, SUB, CMP_STRIDE * 2 * DH), cmp_w2


def _kv_out(z3, c0, rows):
    b = z3.shape[0]
    return z3[:, rows, c0:c0 + 256].reshape(b, -1, 2, NSA_KV, DH)[None]


def kernel(x_prompt, x_sample, c_prompt, c_sample, cache_cmp, cache_sel, state_win, state_hgrn, page_table, w_ada, b_ada, norm_mix_pre, norm_mix_post, norm_ffn_pre, norm_ffn_post, w_in, cmp_pe, cmp_w1, cmp_w2, hg_lb_logits, hg_norm, w_branch_a, w_branch_b, w_out, w_router, b_router, w_up, b_up, w_down, b_down):
    bp, t, d = x_prompt.shape
    bs, ts, _ = x_sample.shape
    depth = w_in.shape[0]
    assert depth == 1 and ts <= SUB
    n_pool = cache_cmp.shape[1]
    past = page_table.shape[1] * PAGE
    lb_all = jnp.cumsum(jax.nn.softmax(hg_lb_logits.astype(F32), axis=0), axis=0)

    l = 0
    w_packed = _pack_w_in(w_in[l])
    w1t, pe_t, w2c = _pack_compress(cmp_pe[l], cmp_w1[l], cmp_w2[l])
    wa, wb, wo = w_branch_a[l].astype(BF16), w_branch_b[l].astype(BF16), w_out[l].astype(BF16)
    wu, wd = w_up[l], w_down[l]

    mod = _modulation(jnp.concatenate([c_prompt, c_sample], axis=0), w_ada[l], b_ada[l])
    mods = [m[:, None, :] for m in jnp.split(mod, 6, axis=-1)]
    sh1, sc1, g1, sh2, sc2, g2 = mods
    p_, s_ = slice(0, bp), slice(bp, bp + bs)

    tm_p = 256 if t % 256 == 0 else t
    zp = _inproj(x_prompt.reshape(bp * t, d), norm_mix_pre[l], sc1[p_], sh1[p_], w_packed, t, tm_p)
    zp3 = zp.reshape(bp, t, ZW)
    nch = t // CMP_STRIDE
    n_slc = -(-t // SLC_BLOCK)
    kvcb_p = _compress_prompt(zp3, w1t, pe_t, w2c)
    tq = 256 if t % 256 == 0 else t
    o_nsa_p = _nsa_prompt(zp3, kvcb_p, _overlap_table(nch, nch - 1, -(-n_slc // SUB) * SUB).T, tq)
    s0_p = jnp.zeros((bp, HG_HEADS, HG_DK, HG_DV), F32)
    chunk = int(np.gcd(t, HG_CHUNK))
    tb_h = 512 if t % 512 == 0 else t
    o_hg_p, hg_state_p = _hgrn(zp3, lb_all[l], hg_norm[l], s0_p, chunk, chunk, tb_h, 16 if chunk % 16 == 0 else chunk)
    tm_m = 512 if t % 512 == 0 else t
    y1_p, h2_p, ti_p, tw_p = _merge(x_prompt.reshape(bp * t, d), o_nsa_p.reshape(bp * t, 512), o_hg_p.reshape(bp * t, 512),
                                    zp, g1[p_], sc2[p_], sh2[p_], wa, wb, wo, norm_mix_post[l], norm_ffn_pre[l],
                                    w_router[l], b_router[l], t, tm_m)

    xs = jnp.pad(x_sample, ((0, 0), (0, SUB - ts), (0, 0))).reshape(bs * SUB, d)
    rep = lambda m: jnp.broadcast_to(m[s_], (bs, SUB, d)).reshape(bs * SUB, d)
    tm_s = 512 if (bs * SUB) % 512 == 0 else bs * SUB
    zs = _inproj(xs, norm_mix_pre[l], rep(sc1), rep(sh1), w_packed, 0, min(tm_s, 256))
    zs3 = zs.reshape(bs, SUB, ZW)
    tok_minor = lambda a: jnp.transpose(a, (0, 2, 3, 4, 1)).reshape(a.shape[0], 2, NSA_KV * DH, a.shape[1])
    o_nsa_s = _nsa_sample(zs3, tok_minor(state_win[l]), tok_minor(cache_cmp[l]), tok_minor(cache_sel[l]),
                          page_table, w1t, pe_t, w2c,
                          _overlap_table(past // CMP_STRIDE, past // CMP_STRIDE - 1, 256),
                          _expand_table(128, past), ts)
    o_hg_s, hg_state_s = _hgrn(zs3, lb_all[l], hg_norm[l], state_hgrn[l], SUB, ts, SUB, SUB)
    y1_s, h2_s, ti_s, tw_s = _merge(xs, o_nsa_s.reshape(bs * SUB, 512), o_hg_s.reshape(bs * SUB, 512), zs,
                                    rep(g1), rep(sc2), rep(sh2), wa, wb, wo, norm_mix_post[l], norm_ffn_pre[l],
                                    w_router[l], b_router[l], 0, tm_s)
    real = lambda a: a.reshape(bs, SUB, -1)[:, :ts].reshape(bs * ts, -1)

    h2 = jnp.concatenate([h2_p, real(h2_s)], axis=0)
    top_i = jnp.concatenate([ti_p[:, :TOP_K], real(ti_s)[:, :TOP_K]], axis=0)
    tm_e = 512
    items, row_token, pos_of = _route_tables(top_i, tm_e)
    y_sorted = _moe_ffn(h2, items, row_token, wu, b_up[l], wd, b_down[l], tm_e)
    np_tok = bp * t
    tm_cp = 256 if t % 256 == 0 else t
    y_p = _combine(pos_of[:np_tok * TOP_K], y_sorted, tw_p, y1_p, g2[p_], norm_ffn_post[l], t, tm_cp).reshape(bp, t, d)
    g2_s = jnp.broadcast_to(g2[s_], (bs, ts, d)).reshape(bs * ts, d)
    tm_cs = 256 if (bs * ts) % 256 == 0 else bs * ts
    y_s = _combine(pos_of[np_tok * TOP_K:], y_sorted, real(tw_s), real(y1_s), g2_s, norm_ffn_post[l], 0,
                   tm_cs).reshape(bs, ts, d)
    keep = min(WINDOW, t)
    new_rows = zs3[:, :ts, C_KW:C_KW + 256].reshape(bs, ts, 2, NSA_KV, DH)
    win_s = jnp.concatenate([state_win[l], new_rows], axis=1)[:, ts:][None]
    return (y_p, y_s,
            _kv_out(zp3, C_KC, slice(0, t)), _kv_out(zs3, C_KC, slice(0, ts)),
            _kv_out(zp3, C_KS, slice(0, t)), _kv_out(zs3, C_KS, slice(0, ts)),
            _kv_out(zp3, C_KW, slice(t - keep, t)), win_s,
            hg_state_p[None], hg_state_s[None])
```
